```python
import math
import jax, jax.numpy as jnp
from jax import lax
import numpy as np

D_MODEL = 4096
BATCH = 4
SEQ = 2048
DEPTH = 1

GRID_W = 64
CTX_LEN = 256
EPS = 1e-6
N_MOD = 6
SSD_WIDTH = D_MODEL
SSD_HEAD_DIM = 64
SSD_HEADS = SSD_WIDTH // SSD_HEAD_DIM
SSD_GROUPS = 8
SSD_STATE = 128
SSD_CONV = 3
SSD_CHUNK = 128
SSD_XBC = SSD_WIDTH + 2 * SSD_GROUPS * SSD_STATE
RWKV_WIDTH = D_MODEL
RWKV_HEAD_DIM = 64
RWKV_HEADS = RWKV_WIDTH // RWKV_HEAD_DIM
DECAY_RANK = 128
ICLR_RANK = 128
GATE_RANK = 480
RWKV_GN_EPS = 64e-5
RW_SIZES = (RWKV_WIDTH, RWKV_WIDTH, RWKV_WIDTH, DECAY_RANK, DECAY_RANK, ICLR_RANK, ICLR_RANK, GATE_RANK)
RWKV_COLS = sum(RW_SIZES)
IN_SIZES = (SSD_WIDTH, SSD_XBC, 2 * SSD_HEADS, RWKV_COLS, 2 * D_MODEL)
IN_COLS = sum(IN_SIZES)
N_EXPERTS = 32
TOP_K = 4
D_EXPERT = 1536
SWIGLU_LIMIT = 7.0
SWIGLU_ALPHA = 1.702
MOE_BLOCK = 128

kernel_name = "hybrid_ssd_rwkv7_moe_dit_block"


def _offsets(sizes):
    return np.cumsum(sizes)[:-1].tolist()


def rmsnorm(x, g):
    xf = x.astype(jnp.float32)
    y = xf * lax.rsqrt(jnp.mean(xf * xf, axis=-1, keepdims=True) + EPS)
    return (y * g.astype(jnp.float32)).astype(x.dtype)


def modulate(h, shift, scale):
    return h * (1.0 + scale) + shift


def dwconv_centred(u, w, b):
    pad = w.shape[0] // 2
    y = lax.conv_general_dilated(u, w[:, None, :].astype(u.dtype), window_strides=(1,),
                                 padding=[(pad, pad)], dimension_numbers=('NWC', 'WIO', 'NWC'),
                                 feature_group_count=u.shape[-1])
    return y + b


def qshift_grid(p):
    b, L, C = p.shape
    rows = L // GRID_W
    g = p.reshape(b, rows, GRID_W, C // 4, 4)
    left = jnp.pad(g[:, :, :-1, :, 0], ((0, 0), (0, 0), (1, 0), (0, 0)))
    right = jnp.pad(g[:, :, 1:, :, 1], ((0, 0), (0, 0), (0, 1), (0, 0)))
    up = jnp.pad(g[:, :-1, :, :, 2], ((0, 0), (1, 0), (0, 0), (0, 0)))
    down = jnp.pad(g[:, 1:, :, :, 3], ((0, 0), (0, 1), (0, 0), (0, 0)))
    return jnp.stack([left, right, up, down], axis=-1).reshape(b, L, C)


def shift_seq(p):
    b, L, C = p.shape
    g = p.reshape(b, L, C // 2, 2)
    prev = jnp.pad(g[:, :-1, :, 0], ((0, 0), (1, 0), (0, 0)))
    nxt = jnp.pad(g[:, 1:, :, 1], ((0, 0), (0, 1), (0, 0)))
    return jnp.stack([prev, nxt], axis=-1).reshape(b, L, C)


def ssd_scan(xh, dt, A, Bm, Cm, D_skip, s0):
    b, L, H, P = xh.shape
    G, N = Bm.shape[2], Bm.shape[3]
    R = H // G
    Q = SSD_CHUNK
    nc = L // Q
    x = xh.reshape(b, nc, Q, G, R, P)
    dtc = dt.reshape(b, nc, Q, G, R)
    a = dtc * A.reshape(G, R)
    xdt = x * dtc[..., None]
    Bc = Bm.reshape(b, nc, Q, G, N)
    Cc = Cm.reshape(b, nc, Q, G, N)
    acs = jnp.cumsum(a, axis=2)
    tril = jnp.tril(jnp.ones((Q, Q), dtype=bool))[:, :, None, None]
    seg = acs[:, :, :, None] - acs[:, :, None, :]
    Lmat = jnp.exp(jnp.where(tril, seg, -jnp.inf))
    CB = jnp.einsum('bcqgn,bckgn->bcqkg', Cc, Bc)
    M = CB[..., None] * Lmat
    y_diag = jnp.einsum('bcqkgr,bckgrp->bcqgrp', M, xdt)
    decay_to_end = jnp.exp(acs[:, :, -1:] - acs)
    states = jnp.einsum('bckgn,bckgrp->bcgrpn', Bc, xdt * decay_to_end[..., None])
    chunk_decay = jnp.exp(acs[:, :, -1])

    def step(s, inp):
        st, dec = inp
        return s * dec[..., None, None] + st, s

    s_fin, s_prev = lax.scan(step, s0.reshape(b, G, R, P, N),
                             (jnp.moveaxis(states, 1, 0), jnp.moveaxis(chunk_decay, 1, 0)))
    s_prev = jnp.moveaxis(s_prev, 0, 1)
    y_off = jnp.einsum('bcqgn,bcgrpn->bcqgrp', Cc, s_prev) * jnp.exp(acs)[..., None]
    y = y_diag + y_off + x * D_skip.reshape(G, R)[:, :, None]
    return y.reshape(b, L, H, P), s_fin.reshape(b, H, P, N)


def ssd_prepare(xbc, dt_raw, lp):
    f32 = jnp.float32
    b, L = xbc.shape[:2]
    xbc = jax.nn.silu(dwconv_centred(xbc, lp['ssd_conv_w'], lp['ssd_conv_b'])).astype(f32)
    xs, Bm, Cm = jnp.split(xbc, [SSD_WIDTH, SSD_WIDTH + SSD_GROUPS * SSD_STATE], axis=-1)
    xs = xs.reshape(b, L, SSD_HEADS, SSD_HEAD_DIM)
    Bm = Bm.reshape(b, L, SSD_GROUPS, SSD_STATE)
    Cm = Cm.reshape(b, L, SSD_GROUPS, SSD_STATE)
    dt_f, dt_b = jnp.split(dt_raw.astype(f32), 2, axis=-1)
    dts = (jax.nn.softplus(dt_f + lp['ssd_dt_bias'][0].astype(f32)),
           jax.nn.softplus(dt_b + lp['ssd_dt_bias'][1].astype(f32)))
    return xs, Bm, Cm, dts


def ssd_bidir(xs, Bm, Cm, dts, lp, s_init):
    outs, finals = [], []
    for d in range(2):
        A = -jnp.exp(lp['ssd_a_log'][d].astype(jnp.float32))
        D_skip = lp['ssd_d'][d].astype(jnp.float32)
        seq = (xs, dts[d], Bm, Cm)
        if d == 1:
            seq = tuple(jnp.flip(t, 1) for t in seq)
        y, s = ssd_scan(seq[0], seq[1], A, seq[2], seq[3], D_skip, s_init[d])
        outs.append(jnp.flip(y, 1) if d == 1 else y)
        finals.append(s)
    return outs[0] + outs[1], finals


def ssd_output(y, z, g):
    b, L = y.shape[:2]
    u = y.reshape(b, L, SSD_WIDTH) * jax.nn.silu(z.astype(jnp.float32))
    u = u.reshape(b, L, SSD_GROUPS, SSD_WIDTH // SSD_GROUPS)
    u = u * lax.rsqrt(jnp.mean(u * u, axis=-1, keepdims=True) + EPS)
    return (u.reshape(b, L, SSD_WIDTH) * g.astype(jnp.float32)).astype(z.dtype)


def rwkv_scan(r, w, k, v, a, bb, s0):
    def step(S, inp):
        r_t, w_t, k_t, v_t, a_t, b_t = inp
        sa = jnp.einsum('bhvk,bhk->bhv', S, a_t)
        S = S * w_t[:, :, None, :] + sa[..., None] * b_t[:, :, None, :] + v_t[..., None] * k_t[:, :, None, :]
        return S, jnp.einsum('bhvk,bhk->bhv', S, r_t)

    seq = tuple(jnp.moveaxis(t, 1, 0) for t in (r, w, k, v, a, bb))
    s_fin, ys = lax.scan(step, s0, seq)
    return jnp.moveaxis(ys, 0, 1), s_fin


def rwkv_prepare(p, shifted, lp):
    f32 = jnp.float32
    b, L = p.shape[:2]
    p = p + lp['rw_mu'] * (shifted - p)
    r, k, v, wlo_f, wlo_b, alo_f, alo_b, glo = jnp.split(p, _offsets(RW_SIZES), axis=-1)
    hs = lambda t: t.reshape(b, L, RWKV_HEADS, RWKV_HEAD_DIM).astype(f32)
    kk = hs(k * lp['rw_k_k'])
    kk = kk / jnp.maximum(jnp.sqrt(jnp.sum(kk * kk, axis=-1, keepdims=True)), 1e-12)
    dirs = []
    for d, (wlo, alo) in enumerate(((wlo_f, wlo_b), (alo_f, alo_b)) and ((wlo_f, alo_f), (wlo_b, alo_b))):
        wlog = -jax.nn.softplus(-(lp['rw_w0'][d] + jnp.tanh(wlo) @ lp['rw_w2'][d]).astype(f32)) - 0.5
        decay = jnp.exp(-jnp.exp(wlog))
        a = jax.nn.sigmoid((lp['rw_a0'][d] + alo @ lp['rw_a2'][d]).astype(f32))
        kd = k.astype(f32) * (1.0 + (a - 1.0) * lp['rw_k_a'].astype(f32))
        dirs.append((hs(decay), hs(kd), hs(a)))
    g = jax.nn.sigmoid(glo) @ lp['rw_g2']
    return hs(r), hs(v), kk, dirs, g


def rwkv_bidir(r, v, kk, dirs, s_init):
    outs, finals = [], []
    for d in range(2):
        decay, kd, a = dirs[d]
        seq = (r, decay, kd, v, -kk, kk * a)
        if d == 1:
            seq = tuple(jnp.flip(t, 1) for t in seq)
        y, s = rwkv_scan(*seq, s_init[d])
        outs.append(jnp.flip(y, 1) if d == 1 else y)
        finals.append(s)
    return outs[0] + outs[1], finals


def rwkv_output(y, r, v, dirs, g, lp):
    b, L = y.shape[:2]
    mu = jnp.mean(y, axis=-1, keepdims=True)
    var = jnp.mean(jnp.square(y - mu), axis=-1, keepdims=True)
    yn = ((y - mu) * lax.rsqrt(var + RWKV_GN_EPS)).reshape(b, L, RWKV_WIDTH)
    yn = yn * lp['rw_ln_g'].astype(jnp.float32) + lp['rw_ln_b'].astype(jnp.float32)
    r_k = lp['rw_r_k'].astype(jnp.float32)
    bonus = (jnp.sum(r * dirs[0][1] * r_k, axis=-1, keepdims=True)
             + jnp.sum(r * dirs[1][1] * r_k, axis=-1, keepdims=True)) * v
    out = (yn + bonus.reshape(b, L, RWKV_WIDTH)) * g.astype(jnp.float32)
    return out.astype(g.dtype)


def merge_branches(y_ssd, y_rw, gates, lp):
    g_s, g_r = jnp.split(jax.nn.sigmoid(gates), 2, axis=-1)
    m = g_s * (y_ssd @ lp['w_br_ssd']) + g_r * (y_rw @ lp['w_br_rw'])
    return m @ lp['w_out']


def mixer_layer(h_lat, h_ctx, lp, with_ctx_out):
    f32 = jnp.float32
    b = h_lat.shape[0]
    P_l = h_lat @ lp['w_in']
    P_c = h_ctx @ lp['w_in']
    z_l, xbc_l, dt_l, rw_l, gt_l = jnp.split(P_l, _offsets(IN_SIZES), axis=-1)
    z_c, xbc_c, dt_c, rw_c, gt_c = jnp.split(P_c, _offsets(IN_SIZES), axis=-1)

    s0_ssd = jnp.zeros((b, SSD_HEADS, SSD_HEAD_DIM, SSD_STATE), f32)
    xs_c, B_c, C_c, dts_c = ssd_prepare(xbc_c, dt_c, lp)
    y_ssd_c, s_ssd = ssd_bidir(xs_c, B_c, C_c, dts_c, lp, (s0_ssd, s0_ssd))
    xs_l, B_l, C_l, dts_l = ssd_prepare(xbc_l, dt_l, lp)
    y_ssd_l, _ = ssd_bidir(xs_l, B_l, C_l, dts_l, lp, s_ssd)

    s0_rw = jnp.zeros((b, RWKV_HEADS, RWKV_HEAD_DIM, RWKV_HEAD_DIM), f32)
    r_c, v_c, kk_c, dirs_c, g_c = rwkv_prepare(rw_c, shift_seq(rw_c), lp)
    y_rw_c, s_rw = rwkv_bidir(r_c, v_c, kk_c, dirs_c, (s0_rw, s0_rw))
    r_l, v_l, kk_l, dirs_l, g_l = rwkv_prepare(rw_l, qshift_grid(rw_l), lp)
    y_rw_l, _ = rwkv_bidir(r_l, v_l, kk_l, dirs_l, s_rw)

    out_l = merge_branches(ssd_output(y_ssd_l, z_l, lp['ssd_norm_g']),
                           rwkv_output(y_rw_l, r_l, v_l, dirs_l, g_l, lp), gt_l, lp)
    out_c = None
    if with_ctx_out:
        out_c = merge_branches(ssd_output(y_ssd_c, z_c, lp['ssd_norm_g']),
                               rwkv_output(y_rw_c, r_c, v_c, dirs_c, g_c, lp), gt_c, lp)
    return out_l, out_c


def moe_ffn(h, lp):
    f32 = jnp.float32
    b, L, D = h.shape
    T = b * L
    xt = h.reshape(T, D)
    logits = (xt @ lp['router_w'] + lp['router_b']).astype(f32)
    top_v, top_i = lax.top_k(logits, TOP_K)
    wts = jax.nn.softmax(top_v, axis=-1)
    n_assign = T * TOP_K
    flat_e = top_i.reshape(-1)
    flat_t = jnp.arange(n_assign, dtype=jnp.int32) // TOP_K
    flat_w = wts.reshape(-1)
    order = jnp.argsort(flat_e)
    sorted_e = flat_e[order]
    counts = jnp.zeros((N_EXPERTS,), jnp.int32).at[flat_e].add(1)
    padded = (counts + MOE_BLOCK - 1) // MOE_BLOCK * MOE_BLOCK
    start_sorted = jnp.cumsum(counts) - counts
    pad_end = jnp.cumsum(padded)
    start_pad = pad_end - padded
    rank = jnp.arange(n_assign, dtype=jnp.int32) - start_sorted[sorted_e]
    dest = start_pad[sorted_e] + rank
    n_blocks = n_assign // MOE_BLOCK + N_EXPERTS
    n_pad = n_blocks * MOE_BLOCK
    slot_tok = jnp.full((n_pad,), T, jnp.int32).at[dest].set(flat_t[order])
    slot_w = jnp.zeros((n_pad,), f32).at[dest].set(flat_w[order])
    block_start = jnp.arange(n_blocks, dtype=jnp.int32) * MOE_BLOCK
    block_e = jnp.minimum(jnp.searchsorted(pad_end, block_start, side='right'), N_EXPERTS - 1)
    x_ext = jnp.concatenate([xt, jnp.zeros((1, D), xt.dtype)], axis=0)
    w_gu, b_gu, w_dn, b_dn = lp['w_gu'], lp['b_gu'], lp['w_down'], lp['b_down']

    def block_step(acc, inp):
        tok, w, e = inp
        xb = x_ext[tok]
        gu = xb @ w_gu[e] + b_gu[e]
        gate, up = jnp.split(gu, 2, axis=-1)
        gate = jnp.minimum(gate, SWIGLU_LIMIT)
        up = jnp.clip(up, -SWIGLU_LIMIT, SWIGLU_LIMIT)
        act = (up + 1.0) * gate * jax.nn.sigmoid(SWIGLU_ALPHA * gate)
        yb = (act @ w_dn[e] + b_dn[e]).astype(f32)
        return acc.at[tok].add(yb * w[:, None]), None

    acc, _ = lax.scan(block_step, jnp.zeros((T + 1, D), f32),
                      (slot_tok.reshape(n_blocks, MOE_BLOCK), slot_w.reshape(n_blocks, MOE_BLOCK), block_e))
    return acc[:T].reshape(b, L, D).astype(h.dtype)


def setup_inputs(seed: int = 0) -> dict:
    key = jax.random.key(seed)
    keys = list(jax.random.split(key, 40))
    f32 = jnp.float32

    def nrm(shape, scale):
        return jax.random.normal(keys.pop(), shape, f32) * scale

    def unif(shape, lo, hi):
        return jax.random.uniform(keys.pop(), shape, f32, lo, hi)

    Dm = D_MODEL
    dt0 = jnp.exp(unif((DEPTH, 2, SSD_HEADS), math.log(1e-3), math.log(1e-1)))
    return {
        "x": nrm((BATCH, SEQ, Dm), 1.0),
        "c": nrm((BATCH, Dm), 1.0),
        "ctx": nrm((BATCH, CTX_LEN, Dm), 1.0),
        "c_ctx": nrm((Dm,), 1.0),
        "ada_w": nrm((DEPTH, Dm, N_MOD * Dm), 0.5 * Dm ** -0.5),
        "ada_b": nrm((DEPTH, N_MOD * Dm), 0.02),
        "norm_mix_g": 1.0 + nrm((DEPTH, Dm), 0.02),
        "norm_ffn_g": 1.0 + nrm((DEPTH, Dm), 0.02),
        "w_in": nrm((DEPTH, Dm, IN_COLS), Dm ** -0.5),
        "ssd_conv_w": nrm((DEPTH, SSD_CONV, SSD_XBC), SSD_CONV ** -0.5),
        "ssd_conv_b": nrm((DEPTH, SSD_XBC), 0.02),
        "ssd_dt_bias": dt0 + jnp.log(-jnp.expm1(-dt0)),
        "ssd_a_log": jnp.log(unif((DEPTH, 2, SSD_HEADS), 1.0, 16.0)),
        "ssd_d": 1.0 + nrm((DEPTH, 2, SSD_HEADS), 0.1),
        "ssd_norm_g": 1.0 + nrm((DEPTH, SSD_WIDTH), 0.02),
        "rw_mu": unif((DEPTH, RWKV_COLS), 0.0, 1.0),
        "rw_w0": unif((DEPTH, 2, RWKV_WIDTH), -6.0, 1.0),
        "rw_w2": nrm((DEPTH, 2, DECAY_RANK, RWKV_WIDTH), 0.1 * DECAY_RANK ** -0.5),
        "rw_a0": nrm((DEPTH, 2, RWKV_WIDTH), 0.5),
        "rw_a2": nrm((DEPTH, 2, ICLR_RANK, RWKV_WIDTH), 0.5 * ICLR_RANK ** -0.5),
        "rw_k_k": 0.85 + nrm((DEPTH, RWKV_WIDTH), 0.05),
        "rw_k_a": 1.0 + nrm((DEPTH, RWKV_WIDTH), 0.05),
        "rw_r_k": nrm((DEPTH, RWKV_HEADS, RWKV_HEAD_DIM), 0.1),
        "rw_ln_g": 1.0 + nrm((DEPTH, RWKV_WIDTH), 0.02),
        "rw_ln_b": nrm((DEPTH, RWKV_WIDTH), 0.02),
        "rw_g2": nrm((DEPTH, GATE_RANK, RWKV_WIDTH), GATE_RANK ** -0.5),
        "w_br_ssd": nrm((DEPTH, SSD_WIDTH, Dm), SSD_WIDTH ** -0.5),
        "w_br_rw": nrm((DEPTH, RWKV_WIDTH, Dm), RWKV_WIDTH ** -0.5),
        "w_out": nrm((DEPTH, Dm, Dm), Dm ** -0.5),
        "router_w": nrm((DEPTH, Dm, N_EXPERTS), Dm ** -0.5),
        "router_b": nrm((DEPTH, N_EXPERTS), 0.01),
        "w_gu": nrm((DEPTH, N_EXPERTS, Dm, 2 * D_EXPERT), Dm ** -0.5),
        "b_gu": nrm((DEPTH, N_EXPERTS, 2 * D_EXPERT), 0.01),
        "w_down": nrm((DEPTH, N_EXPERTS, D_EXPERT, Dm), D_EXPERT ** -0.5),
        "b_down": nrm((DEPTH, N_EXPERTS, Dm), 0.01),
        "final_g": 1.0 + nrm((Dm,), 0.02),
    }


def reference(x, c, ctx, c_ctx, ada_w, ada_b, norm_mix_g, norm_ffn_g, w_in, ssd_conv_w, ssd_conv_b,
              ssd_dt_bias, ssd_a_log, ssd_d, ssd_norm_g, rw_mu, rw_w0, rw_w2, rw_a0, rw_a2, rw_k_k,
              rw_k_a, rw_r_k, rw_ln_g, rw_ln_b, rw_g2, w_br_ssd, w_br_rw, w_out, router_w, router_b,
              w_gu, b_gu, w_down, b_down, final_g):
    lat, cx = x, ctx
    cond_l = jax.nn.silu(c)[:, None, :]
    cond_c = jax.nn.silu(c_ctx)[None, None, :]
    for l in range(DEPTH):
        lp = dict(w_in=w_in[l], ssd_conv_w=ssd_conv_w[l], ssd_conv_b=ssd_conv_b[l],
                  ssd_dt_bias=ssd_dt_bias[l], ssd_a_log=ssd_a_log[l], ssd_d=ssd_d[l],
                  ssd_norm_g=ssd_norm_g[l], rw_mu=rw_mu[l], rw_w0=rw_w0[l], rw_w2=rw_w2[l],
                  rw_a0=rw_a0[l], rw_a2=rw_a2[l], rw_k_k=rw_k_k[l], rw_k_a=rw_k_a[l],
                  rw_r_k=rw_r_k[l], rw_ln_g=rw_ln_g[l], rw_ln_b=rw_ln_b[l], rw_g2=rw_g2[l],
                  w_br_ssd=w_br_ssd[l], w_br_rw=w_br_rw[l], w_out=w_out[l],
                  router_w=router_w[l], router_b=router_b[l], w_gu=w_gu[l], b_gu=b_gu[l],
                  w_down=w_down[l], b_down=b_down[l])
        sh1_l, sc1_l, g1_l, sh2_l, sc2_l, g2_l = jnp.split(cond_l @ ada_w[l] + ada_b[l], N_MOD, axis=-1)
        sh1_c, sc1_c, g1_c, sh2_c, sc2_c, g2_c = jnp.split(cond_c @ ada_w[l] + ada_b[l], N_MOD, axis=-1)
        last = l == DEPTH - 1
        h_l = modulate(rmsnorm(lat, norm_mix_g[l]), sh1_l, sc1_l)
        h_c = modulate(rmsnorm(cx, norm_mix_g[l]), sh1_c, sc1_c)
        out_l, out_c = mixer_layer(h_l, h_c, lp, not last)
        lat = lat + g1_l * out_l
        lat = lat + g2_l * moe_ffn(modulate(rmsnorm(lat, norm_ffn_g[l]), sh2_l, sc2_l), lp)
        if not last:
            cx = cx + g1_c * out_c
            cx = cx + g2_c * moe_ffn(modulate(rmsnorm(cx, norm_ffn_g[l]), sh2_c, sc2_c), lp)
    return rmsnorm(lat, final_g)
```

```python
import functools

import jax
import jax.numpy as jnp
from jax import lax
from jax.experimental import pallas as pl
from jax.experimental.pallas import tpu as pltpu

F32 = jnp.float32
BF16 = jnp.bfloat16

HEAD_DIM = 64
SSD_GROUPS = 8
SSD_STATE = 128
SSD_CHUNK = 128
RW_CHUNK = 64
RW_LANES = 256
GRID_W = 64
LOW_RANK = 128
EPS = 1e-6
RWKV_GN_EPS = 64e-5
TOP_K = 4
SWIGLU_LIMIT = 7.0
SWIGLU_ALPHA = 1.702
MOE_TM = 256
VMEM_LIMIT_V7X = 56 * 1024 * 1024

NT_DIMS = (((1,), (1,)), ((), ()))
TN_DIMS = (((0,), (0,)), ((), ()))


def _cparams(sem):
    return pltpu.CompilerParams(dimension_semantics=sem, vmem_limit_bytes=VMEM_LIMIT_V7X)


def _dot(a, b):
    return jnp.dot(a, b, preferred_element_type=F32)


def _dotg(a, b, dims):
    return lax.dot_general(a, b, dims, preferred_element_type=F32)


def _split2(x):
    hi = x.astype(BF16)
    lo = (x - hi.astype(F32)).astype(BF16)
    return hi, lo


def _dot_x2(x, e):
    hi, lo = _split2(x)
    return _dot(hi, e) + _dot(lo, e)


def _dot_x3_left(m, x):
    hi = x.astype(BF16)
    r1 = x - hi.astype(F32)
    mid = r1.astype(BF16)
    lo = (r1 - mid.astype(F32)).astype(BF16)
    return _dot(m, hi) + _dot(m, mid) + _dot(m, lo)


def _dot_x3_right_nt(x, m):
    hi = x.astype(BF16)
    r1 = x - hi.astype(F32)
    mid = r1.astype(BF16)
    lo = (r1 - mid.astype(F32)).astype(BF16)
    return _dotg(hi, m, NT_DIMS) + _dotg(mid, m, NT_DIMS) + _dotg(lo, m, NT_DIMS)


def _softplus(x):
    return jnp.maximum(x, 0.0) + jnp.log1p(jnp.exp(-jnp.abs(x)))


def _silu(x):
    return x * jax.nn.sigmoid(x)


def _iota(shape, dim):
    return lax.broadcasted_iota(jnp.int32, shape, dim)


def _pick_tile(n, prefs):
    for p in prefs:
        if n % p == 0:
            return p
    return n


def _mm_kernel(a_ref, w_ref, bias_ref, o_ref, *scratch):
    if scratch:
        wb_ref, = scratch

        @pl.when(pl.program_id(1) == 0)
        def _():
            wb_ref[...] = w_ref[...].astype(BF16)
        w = wb_ref[...]
    else:
        w = w_ref[...]
    o_ref[...] = (_dot(a_ref[...], w) + bias_ref[...]).astype(o_ref.dtype)


def _matmul(a, w, bias, out_dtype, tm, tn):
    m, k = a.shape
    n = w.shape[1]
    scratch = [pltpu.VMEM((k, tn), BF16)] if w.dtype != BF16 else []
    return pl.pallas_call(
        _mm_kernel,
        grid=(n // tn, m // tm),
        in_specs=[pl.BlockSpec((tm, k), lambda j, i: (i, 0)),
                  pl.BlockSpec((k, tn), lambda j, i: (0, j)),
                  pl.BlockSpec((1, tn), lambda j, i: (0, j))],
        out_specs=pl.BlockSpec((tm, tn), lambda j, i: (i, j)),
        out_shape=jax.ShapeDtypeStruct((m, n), out_dtype),
        scratch_shapes=scratch,
        compiler_params=_cparams(("arbitrary", "arbitrary")),
        name="dense_matmul",
    )(a, w, bias)


def _norm_mod_kernel(x_ref, g_ref, sh_ref, sc_ref, o_ref):
    x = x_ref[0]
    ms = jnp.mean(x * x, axis=-1, keepdims=True)
    y = x * lax.rsqrt(ms + EPS) * g_ref[...]
    o_ref[0] = (y * (1.0 + sc_ref[0]) + sh_ref[0]).astype(o_ref.dtype)


def _norm_mod(x, g, shift, scale, out_dtype):
    b, l, d = x.shape
    tl = _pick_tile(l, (256, 128, 64))
    per_batch = shift.shape[0] == b
    mod_map = (lambda bi, i: (bi, 0, 0)) if per_batch else (lambda bi, i: (0, 0, 0))
    return pl.pallas_call(
        _norm_mod_kernel,
        grid=(b, l // tl),
        in_specs=[pl.BlockSpec((1, tl, d), lambda bi, i: (bi, i, 0)),
                  pl.BlockSpec((1, d), lambda bi, i: (0, 0)),
                  pl.BlockSpec((1, 1, d), mod_map),
                  pl.BlockSpec((1, 1, d), mod_map)],
        out_specs=pl.BlockSpec((1, tl, d), lambda bi, i: (bi, i, 0)),
        out_shape=jax.ShapeDtypeStruct((b, l, d), out_dtype),
        compiler_params=_cparams(("arbitrary", "arbitrary")),
        name="norm_modulate",
    )(x, g, shift, scale)


def _conv_segment(u, w, bias):
    n = u.shape[0]
    row = _iota(u.shape, 0)
    prev = jnp.where(row == 0, 0.0, pltpu.roll(u, 1, 0))
    nxt = jnp.where(row == n - 1, 0.0, pltpu.roll(u, n - 1, 0))
    y = w[0:1] * prev + w[1:2] * u + w[2:3] * nxt + bias
    return _silu(y)


def _conv_kernel(u_ref, w_ref, b_ref, o_ref, *, ctx):
    w = w_ref[...]
    bias = b_ref[...]
    t = u_ref.shape[1]
    o_ref[0, 0:ctx, :] = _conv_segment(u_ref[0, 0:ctx, :].astype(F32), w, bias).astype(o_ref.dtype)
    o_ref[0, ctx:t, :] = _conv_segment(u_ref[0, ctx:t, :].astype(F32), w, bias).astype(o_ref.dtype)


def _ssd_conv(p_big, conv_w, conv_b, d, xbc, ctx):
    b, t, _ = p_big.shape
    ct = 256
    off = d // ct
    return pl.pallas_call(
        functools.partial(_conv_kernel, ctx=ctx),
        grid=(b, xbc // ct),
        in_specs=[pl.BlockSpec((1, t, ct), lambda bi, j: (bi, 0, off + j)),
                  pl.BlockSpec((3, ct), lambda bi, j: (0, j)),
                  pl.BlockSpec((1, ct), lambda bi, j: (0, j))],
        out_specs=pl.BlockSpec((1, t, ct), lambda bi, j: (bi, 0, j)),
        out_shape=jax.ShapeDtypeStruct((b, t, xbc), BF16),
        compiler_params=_cparams(("arbitrary", "arbitrary")),
        name="ssd_conv_silu",
    )(p_big, conv_w, conv_b)


def _ssd_kernel(x_ref, b_ref, c_ref, dt_ref, dtt_ref, pc_ref, pr_ref, y_ref, st_ref, *, nq, r, h):
    q = SSD_CHUNK
    rp = r * HEAD_DIM
    nl = dt_ref.shape[2]
    gi = pl.program_id(1)
    d = pl.program_id(2)
    s = pl.program_id(3)
    fwd = d == 0
    base = d * h + gi * r

    @pl.when(s == 0)
    def _():
        st_ref[...] = jnp.zeros_like(st_ref)

    rowi = _iota((q, q), 0)
    coli = _iota((q, q), 1)
    sgn = 1 - 2 * d
    incl = (rowi - coli) * sgn >= 0
    tri = incl.astype(BF16)
    expand = (_iota((nl, rp), 0) == base + _iota((nl, rp), 1) // HEAD_DIM).astype(BF16)
    lane_head = _iota((1, 2 * HEAD_DIM), 1) // HEAD_DIM
    lane_c = _iota((q, nl), 1)
    row_r = _iota((nl, q), 0)

    bias_c, a_c, dsk_c = pc_ref[0:1, :], pc_ref[1:2, :], pc_ref[2:3, :]
    bias_r, a_r = pr_ref[0], pr_ref[1]
    dsk_x = _dot_x2(jnp.broadcast_to(dsk_c, (8, nl)), expand)[0:1]

    for j in range(nq):
        jj = jnp.where(fwd, j, nq - 1 - j)
        off = pl.multiple_of(jj * q, q)
        x = x_ref[0, pl.ds(off, q), :].astype(F32)
        bc = b_ref[0, pl.ds(off, q), :]
        cc = c_ref[0, pl.ds(off, q), :]
        dt_c = _softplus(dt_ref[0, pl.ds(off, q), :] + bias_c)
        dt_r = _softplus(dtt_ref[0, jj] + bias_r)
        al_c = dt_c * a_c
        al_r = dt_r * a_r
        acs_c = _dot_x3_left(tri, al_c)
        acs_r = _dot_x3_right_nt(al_r, tri)
        tot_c = jnp.sum(al_c, axis=0, keepdims=True)

        xdt = x * _dot_x2(dt_c, expand)
        cb = _dotg(cc, bc, NT_DIMS)
        pieces = []
        for pair in range(r // 2):
            xp = xdt[:, pair * 2 * HEAD_DIM:(pair + 1) * 2 * HEAD_DIM]
            acc = None
            for hh in range(2):
                hcol = base + pair * 2 + hh
                col = jnp.sum(jnp.where(lane_c == hcol, acs_c, 0.0), axis=1, keepdims=True)
                row = jnp.sum(jnp.where(row_r == hcol, acs_r, 0.0), axis=0, keepdims=True)
                lmat = jnp.exp(jnp.where(incl, col - row, -jnp.inf))
                m = (cb * lmat).astype(BF16)
                xm = jnp.where(lane_head == hh, xp, 0.0).astype(BF16)
                term = _dot(m, xm)
                acc = term if acc is None else acc + term
            pieces.append(acc)
        y_diag = pieces[0] if len(pieces) == 1 else jnp.concatenate(pieces, axis=1)

        st = st_ref[...]
        y_off = _dot(cc, st.astype(BF16)) * _dot_x2(jnp.exp(acs_c), expand)
        y = y_diag + y_off + x * dsk_x
        y_ref[0, 0, pl.ds(off, q), :] = y.astype(y_ref.dtype)

        dec = _dot_x2(jnp.exp(tot_c - acs_c), expand)
        upd = _dotg(bc, (xdt * dec).astype(BF16), TN_DIMS)
        end = _dot_x2(jnp.broadcast_to(jnp.exp(tot_c), (8, nl)), expand)[0:1]
        st_ref[...] = st * end + upd


def _block_order(s, d, nctx, nblk):
    bwd = jnp.where(s < nctx, nctx - 1 - s, nblk - 1 - (s - nctx))
    return jnp.where(d == 0, s, bwd)


def _ssd_scan(xbc_act, p_small, dt_col0, dtt, pc, pr, d, tb, nctx):
    b, t, _ = xbc_act.shape
    g = SSD_GROUPS
    h = d // HEAD_DIM
    r = h // g
    rp = r * HEAD_DIM
    nblk = t // tb
    nq = tb // SSD_CHUNK
    n = SSD_STATE
    nl = pc.shape[1]
    blk = lambda s, dd: _block_order(s, dd, nctx, nblk)
    return pl.pallas_call(
        functools.partial(_ssd_kernel, nq=nq, r=r, h=h),
        grid=(b, g, 2, nblk),
        in_specs=[
            pl.BlockSpec((1, tb, rp), lambda bi, gi, dd, s: (bi, blk(s, dd), gi)),
            pl.BlockSpec((1, tb, n), lambda bi, gi, dd, s: (bi, blk(s, dd), d // n + gi)),
            pl.BlockSpec((1, tb, n), lambda bi, gi, dd, s: (bi, blk(s, dd), d // n + g + gi)),
            pl.BlockSpec((1, tb, nl), lambda bi, gi, dd, s: (bi, blk(s, dd), dt_col0 // nl)),
            pl.BlockSpec((1, nq, nl, SSD_CHUNK), lambda bi, gi, dd, s: (bi, blk(s, dd), 0, 0)),
            pl.BlockSpec((3, nl), lambda bi, gi, dd, s: (0, 0)),
            pl.BlockSpec((2, nl, SSD_CHUNK), lambda bi, gi, dd, s: (0, 0, 0)),
        ],
        out_specs=pl.BlockSpec((1, 1, tb, rp), lambda bi, gi, dd, s: (dd, bi, blk(s, dd), gi)),
        out_shape=jax.ShapeDtypeStruct((2, b, t, d), BF16),
        scratch_shapes=[pltpu.VMEM((n, rp), F32)],
        compiler_params=_cparams(("arbitrary",) * 4),
        name="ssd_scan",
    )(xbc_act, xbc_act, xbc_act, p_small, dtt, pc, pr)


def _ssd_out_kernel(y_ref, z_ref, g_ref, o_ref):
    y = y_ref[0, 0].astype(F32) + y_ref[1, 0].astype(F32)
    u = y * _silu(z_ref[0].astype(F32))
    u = u * lax.rsqrt(jnp.mean(u * u, axis=-1, keepdims=True) + EPS)
    o_ref[0] = (u * g_ref[...]).astype(o_ref.dtype)


def _ssd_output(y_ssd, p_big, norm_g, d, ctx, seq):
    _, b, t, _ = y_ssd.shape
    gw = d // SSD_GROUPS
    tl = _pick_tile(seq, (256, 128, 64))
    tl = tl if ctx % tl == 0 else _pick_tile(ctx, (128, 64))
    o = ctx // tl
    return pl.pallas_call(
        _ssd_out_kernel,
        grid=(b, seq // tl, SSD_GROUPS),
        in_specs=[pl.BlockSpec((2, 1, tl, gw), lambda bi, i, gi: (0, bi, o + i, gi)),
                  pl.BlockSpec((1, tl, gw), lambda bi, i, gi: (bi, o + i, gi)),
                  pl.BlockSpec((1, gw), lambda bi, i, gi: (0, gi))],
        out_specs=pl.BlockSpec((1, tl, gw), lambda bi, i, gi: (bi, i, gi)),
        out_shape=jax.ShapeDtypeStruct((b, seq, d), BF16),
        compiler_params=_cparams(("arbitrary",) * 3),
        name="ssd_gated_norm",
    )(y_ssd, p_big, norm_g)


def _shift_kernel(u_ref, mu_ref, o_ref, *, ctx):
    t = u_ref.shape[1]
    mu = mu_ref[...]
    uc = u_ref[0, 0:ctx, :].astype(F32)
    row = _iota(uc.shape, 0)
    lane = _iota(uc.shape, 1)
    prev = jnp.where(row == 0, 0.0, pltpu.roll(uc, 1, 0))
    nxt = jnp.where(row == ctx - 1, 0.0, pltpu.roll(uc, ctx - 1, 0))
    sh = jnp.where(lane % 2 == 0, prev, nxt)
    o_ref[0, 0:ctx, :] = (uc + mu * (sh - uc)).astype(o_ref.dtype)
    ul = u_ref[0, ctx:t, :].astype(F32)
    n = t - ctx
    row = _iota(ul.shape, 0)
    lane = _iota(ul.shape, 1)
    col = row % GRID_W
    left = jnp.where(col == 0, 0.0, pltpu.roll(ul, 1, 0))
    right = jnp.where(col == GRID_W - 1, 0.0, pltpu.roll(ul, n - 1, 0))
    up = jnp.where(row < GRID_W, 0.0, pltpu.roll(ul, GRID_W, 0))
    down = jnp.where(row >= n - GRID_W, 0.0, pltpu.roll(ul, n - GRID_W, 0))
    c4 = lane % 4
    sh = jnp.where(c4 == 0, left, jnp.where(c4 == 1, right, jnp.where(c4 == 2, up, down)))
    o_ref[0, ctx:t, :] = (ul + mu * (sh - ul)).astype(o_ref.dtype)


def _token_shift(p, mu, col0, width, ctx, out_dtype):
    b, t, _ = p.shape
    ct = 256
    off = col0 // ct
    return pl.pallas_call(
        functools.partial(_shift_kernel, ctx=ctx),
        grid=(b, width // ct),
        in_specs=[pl.BlockSpec((1, t, ct), lambda bi, j: (bi, 0, off + j)),
                  pl.BlockSpec((1, ct), lambda bi, j: (0, j))],
        out_specs=pl.BlockSpec((1, t, ct), lambda bi, j: (bi, 0, j)),
        out_shape=jax.ShapeDtypeStruct((b, t, width), out_dtype),
        compiler_params=_cparams(("arbitrary", "arbitrary")),
        name="rwkv_token_shift",
    )(p, mu)


def _rwkv_kernel(r_ref, k_ref, v_ref, wlo_ref, alo_ref, w0_ref, a0_ref, w2_ref, a2_ref,
                 kk_ref, ka_ref, rk_ref, y_ref, bv_ref, s_ref, *, nch):
    c = RW_CHUNK
    w = RW_LANES
    hb = w // HEAD_DIM
    n = hb * c
    d = pl.program_id(1)
    s = pl.program_id(3)
    fwd = d == 0

    @pl.when(s == 0)
    def _():
        s_ref[...] = jnp.zeros_like(s_ref)

    rs = _iota((n, w), 0) // c
    hm = rs == _iota((n, w), 1) // HEAD_DIM
    ri = _iota((n, n), 0)
    ci = _iota((n, n), 1)
    same = (ri // c) == (ci // c)
    sgn = 1 - 2 * d
    before = (ri - ci) * sgn > 0
    strict = same & before
    incl = same & (before | (ci == ri))
    eye = (ri == ci).astype(F32)
    ti = _iota((c, c), 0)
    tj = _iota((c, c), 1)
    tri = ((ti - tj) * sgn >= 0).astype(BF16)
    bmw = (_iota((w, w), 0) // HEAD_DIM) == (_iota((w, w), 1) // HEAD_DIM)
    eblk = bmw.astype(BF16)

    w0 = w0_ref[0]
    a0 = a0_ref[0]
    w2 = w2_ref[0].astype(BF16)
    a2 = a2_ref[0].astype(BF16)
    k_k = kk_ref[...]
    k_a = ka_ref[...]
    r_k = rk_ref[...]

    def tile(x):
        return jnp.concatenate([x] * hb, axis=0)

    def unstack(z):
        out = z[0:c]
        for h in range(1, hb):
            out = out + z[h * c:(h + 1) * c]
        return out

    def body(j, carry):
        jj = jnp.where(fwd, j, nch - 1 - j)
        off = pl.multiple_of(jj * c, c)
        r = r_ref[0, pl.ds(off, c), :].astype(F32)
        k = k_ref[0, pl.ds(off, c), :].astype(F32)
        v = v_ref[0, pl.ds(off, c), :].astype(F32)
        wlo = wlo_ref[0, pl.ds(off, c), :]
        alo = alo_ref[0, pl.ds(off, c), :]

        wl = w0 + _dot(jnp.tanh(wlo).astype(BF16), w2)
        logw = -jnp.exp(-_softplus(-wl) - 0.5)
        asig = jax.nn.sigmoid(a0 + _dot(alo.astype(BF16), a2))
        kkr = k * k_k
        kk = kkr / jnp.maximum(jnp.sqrt(_dot_x2(kkr * kkr, eblk)), 1e-12)
        kd = k * (1.0 + (asig - 1.0) * k_a)
        kb = kk * asig
        bon = _dot_x2(r * kd * r_k, eblk) * v

        cs = _dot_x3_left(tri, logw)
        tot = jnp.sum(logw, axis=0, keepdims=True)
        e_neg = jnp.exp(-cs)
        e_end = jnp.exp(tot - cs)
        at = -kk * jnp.exp(cs - logw)
        bt = kb * e_neg
        kt = kd * e_neg
        rt = r * jnp.exp(cs)

        l1 = jnp.where(hm, tile(at), 0.0).astype(BF16)
        l2 = jnp.where(hm, tile(rt), 0.0).astype(BF16)
        xb = tile(bt).astype(BF16)
        xk = tile(kt).astype(BF16)
        nab = jnp.where(strict, _dotg(l1, xb, NT_DIMS), 0.0)
        aak = jnp.where(strict, _dotg(l1, xk, NT_DIMS), 0.0).astype(BF16)
        arb = jnp.where(incl, _dotg(l2, xb, NT_DIMS), 0.0).astype(BF16)
        ark = jnp.where(incl, _dotg(l2, xk, NT_DIMS), 0.0).astype(BF16)

        tinv = eye + nab
        npow = nab
        steps = c.bit_length() - 2
        for _ in range(steps):
            nb = npow.astype(BF16)
            npow = _dot(nb, nb)
            tinv = tinv + _dot(tinv.astype(BF16), npow.astype(BF16))

        sb = s_ref[...].astype(BF16)
        vbd = jnp.where(hm, tile(v), 0.0).astype(BF16)
        rhs = jnp.where(hm, tile(_dotg(at.astype(BF16), sb, NT_DIMS)), 0.0) + _dot(aak, vbd)
        ubd = _dot(tinv.astype(BF16), rhs.astype(BF16))
        ybd = _dot(arb, ubd.astype(BF16)) + _dot(ark, vbd)
        u = unstack(ubd)
        y = unstack(ybd) + _dotg(rt.astype(BF16), sb, NT_DIMS)
        y_ref[0, 0, pl.ds(off, c), :] = y.astype(y_ref.dtype)
        bv_ref[0, 0, pl.ds(off, c), :] = bon.astype(bv_ref.dtype)

        uv = jnp.concatenate([u, v], axis=0).astype(BF16)
        bk = jnp.concatenate([kb * e_end, kd * e_end], axis=0).astype(BF16)
        upd = _dotg(uv, bk, TN_DIMS)
        s_ref[...] = s_ref[...] * jnp.exp(tot) + jnp.where(bmw, upd, 0.0)
        return carry

    lax.fori_loop(0, nch, body, 0)


def _rwkv_scan(rkv, low, w0, a0, w2, a2, k_k, k_a, r_k, d, tb, nctx):
    b, t, _ = rkv.shape
    w = RW_LANES
    hg = d // w
    nblk = t // tb
    nch = tb // RW_CHUNK
    dw = d // w
    blk = lambda s, dd: _block_order(s, dd, nctx, nblk)
    tok = lambda cb: pl.BlockSpec((1, tb, w), lambda bi, dd, hi, s: (bi, blk(s, dd), cb + hi))
    par = lambda: pl.BlockSpec((1, w), lambda bi, dd, hi, s: (0, hi))
    pard = lambda: pl.BlockSpec((1, 1, w), lambda bi, dd, hi, s: (dd, 0, hi))
    lowrank = lambda: pl.BlockSpec((1, LOW_RANK, w), lambda bi, dd, hi, s: (dd, 0, hi))
    out = pl.BlockSpec((1, 1, tb, w), lambda bi, dd, hi, s: (dd, bi, blk(s, dd), hi))
    return pl.pallas_call(
        functools.partial(_rwkv_kernel, nch=nch),
        grid=(b, 2, hg, nblk),
        in_specs=[tok(0), tok(dw), tok(2 * dw),
                  pl.BlockSpec((1, tb, LOW_RANK), lambda bi, dd, hi, s: (bi, blk(s, dd), dd)),
                  pl.BlockSpec((1, tb, LOW_RANK), lambda bi, dd, hi, s: (bi, blk(s, dd), 2 + dd)),
                  pard(), pard(), lowrank(), lowrank(), par(), par(), par()],
        out_specs=[out, out],
        out_shape=[jax.ShapeDtypeStruct((2, b, t, d), BF16)] * 2,
        scratch_shapes=[pltpu.VMEM((w, w), F32)],
        compiler_params=_cparams(("arbitrary",) * 4),
        name="rwkv7_scan",
    )(rkv, rkv, rkv, low, low, w0, a0, w2, a2, k_k, k_a, r_k)


def _rwkv_out_kernel(y_ref, bv_ref, glo_ref, g2_ref, lng_ref, lnb_ref, o_ref):
    tn = o_ref.shape[2]
    eblk = ((_iota((tn, tn), 0) // HEAD_DIM) == (_iota((tn, tn), 1) // HEAD_DIM)).astype(BF16)
    y = y_ref[0, 0].astype(F32) + y_ref[1, 0].astype(F32)
    mu = _dot_x2(y, eblk) * (1.0 / HEAD_DIM)
    yc = y - mu
    var = _dot_x2(yc * yc, eblk) * (1.0 / HEAD_DIM)
    yn = yc * lax.rsqrt(var + RWKV_GN_EPS) * lng_ref[...] + lnb_ref[...]
    g = _dot(jax.nn.sigmoid(glo_ref[0]).astype(BF16), g2_ref[...].astype(BF16))
    bonus = bv_ref[0, 0].astype(F32) + bv_ref[1, 0].astype(F32)
    o_ref[0] = ((yn + bonus) * g).astype(o_ref.dtype)


def _rwkv_output(y_rw, bv, low, g2p, ln_g, ln_b, d, ctx, seq):
    _, b, t, _ = y_rw.shape
    tn = 256
    tl = _pick_tile(seq, (256, 128, 64))
    tl = tl if ctx % tl == 0 else _pick_tile(ctx, (128, 64))
    o = ctx // tl
    gr = g2p.shape[0]
    return pl.pallas_call(
        _rwkv_out_kernel,
        grid=(b, seq // tl, d // tn),
        in_specs=[pl.BlockSpec((2, 1, tl, tn), lambda bi, i, j: (0, bi, o + i, j)),
                  pl.BlockSpec((2, 1, tl, tn), lambda bi, i, j: (0, bi, o + i, j)),
                  pl.BlockSpec((1, tl, gr), lambda bi, i, j: (bi, o + i, 1)),
                  pl.BlockSpec((gr, tn), lambda bi, i, j: (0, j)),
                  pl.BlockSpec((1, tn), lambda bi, i, j: (0, j)),
                  pl.BlockSpec((1, tn), lambda bi, i, j: (0, j))],
        out_specs=pl.BlockSpec((1, tl, tn), lambda bi, i, j: (bi, i, j)),
        out_shape=jax.ShapeDtypeStruct((b, seq, d), BF16),
        compiler_params=_cparams(("arbitrary",) * 3),
        name="rwkv_groupnorm_gate",
    )(y_rw, bv, low, g2p, ln_g, ln_b)


def _merge_kernel(a1_ref, a2_ref, w1_ref, w2_ref, gs_ref, gr_ref, o_ref, w1b, w2b):
    @pl.when((pl.program_id(1) == 0) & (pl.program_id(2) == 0))
    def _():
        w1b[...] = w1_ref[...].astype(BF16)
        w2b[...] = w2_ref[...].astype(BF16)
    gs = jax.nn.sigmoid(gs_ref[0].astype(F32))
    gr = jax.nn.sigmoid(gr_ref[0].astype(F32))
    m = gs * _dot(a1_ref[0], w1b[...]) + gr * _dot(a2_ref[0], w2b[...])
    o_ref[0] = m.astype(o_ref.dtype)


def _merge(a1, a2, w1, w2, p_big, gate_col0, ctx):
    b, l, d = a1.shape
    tn = 256
    tm = _pick_tile(l, (512, 256, 128, 64))
    tm = tm if ctx % tm == 0 else _pick_tile(ctx, (256, 128, 64))
    o = ctx // tm
    gs0 = gate_col0 // tn
    gr0 = (gate_col0 + d) // tn
    return pl.pallas_call(
        _merge_kernel,
        grid=(d // tn, b, l // tm),
        in_specs=[pl.BlockSpec((1, tm, d), lambda j, bi, i: (bi, i, 0)),
                  pl.BlockSpec((1, tm, d), lambda j, bi, i: (bi, i, 0)),
                  pl.BlockSpec((d, tn), lambda j, bi, i: (0, j)),
                  pl.BlockSpec((d, tn), lambda j, bi, i: (0, j)),
                  pl.BlockSpec((1, tm, tn), lambda j, bi, i: (bi, o + i, gs0 + j)),
                  pl.BlockSpec((1, tm, tn), lambda j, bi, i: (bi, o + i, gr0 + j))],
        out_specs=pl.BlockSpec((1, tm, tn), lambda j, bi, i: (bi, i, j)),
        out_shape=jax.ShapeDtypeStruct((b, l, d), BF16),
        scratch_shapes=[pltpu.VMEM((d, tn), BF16), pltpu.VMEM((d, tn), BF16)],
        compiler_params=_cparams(("arbitrary",) * 3),
        name="branch_merge",
    )(a1, a2, w1, w2, p_big, p_big)


def _outproj_kernel(m_ref, w_ref, x_ref, g_ref, o_ref, wb):
    @pl.when((pl.program_id(1) == 0) & (pl.program_id(2) == 0))
    def _():
        wb[...] = w_ref[...].astype(BF16)
    o_ref[0] = x_ref[0] + g_ref[0] * _dot(m_ref[0], wb[...])


def _outproj(m, w, x, gate):
    b, l, d = m.shape
    tn = 256
    tm = _pick_tile(l, (512, 256, 128, 64))
    return pl.pallas_call(
        _outproj_kernel,
        grid=(d // tn, b, l // tm),
        in_specs=[pl.BlockSpec((1, tm, d), lambda j, bi, i: (bi, i, 0)),
                  pl.BlockSpec((d, tn), lambda j, bi, i: (0, j)),
                  pl.BlockSpec((1, tm, tn), lambda j, bi, i: (bi, i, j)),
                  pl.BlockSpec((1, 1, tn), lambda j, bi, i: (bi, 0, j))],
        out_specs=pl.BlockSpec((1, tm, tn), lambda j, bi, i: (bi, i, j)),
        out_shape=jax.ShapeDtypeStruct((b, l, d), F32),
        scratch_shapes=[pltpu.VMEM((d, tn), BF16)],
        compiler_params=_cparams(("arbitrary",) * 3),
        name="out_proj_residual",
    )(m, w, x, gate)


def _router_kernel(h_ref, w_ref, b_ref, idx_ref, wt_ref):
    logits = _dot(h_ref[...].astype(BF16), w_ref[...]) + b_ref[...]
    lane = _iota(logits.shape, 1)
    nl = logits.shape[1]
    lane_f = lane.astype(F32)
    work = logits
    idx_out = jnp.zeros(logits.shape, jnp.int32)
    val_out = jnp.zeros(logits.shape, F32)
    v0 = None
    for j in range(TOP_K):
        m = jnp.max(work, axis=-1, keepdims=True)
        sel = jnp.min(jnp.where(work == m, lane_f, float(nl)), axis=-1,
                      keepdims=True).astype(jnp.int32)
        v0 = m if v0 is None else v0
        idx_out = jnp.where(lane == j, sel, idx_out)
        val_out = jnp.where(lane == j, jnp.exp(m - v0), val_out)
        work = jnp.where(lane == sel, -jnp.inf, work)
    idx_ref[...] = idx_out
    wt_ref[...] = val_out / jnp.sum(val_out, axis=-1, keepdims=True)


def _router(h2, rw_pad, rb_pad):
    t, d = h2.shape
    tm = _pick_tile(t, (512, 256, 128))
    nl = rw_pad.shape[1]
    return pl.pallas_call(
        _router_kernel,
        grid=(t // tm,),
        in_specs=[pl.BlockSpec((tm, d), lambda i: (i, 0)),
                  pl.BlockSpec((d, nl), lambda i: (0, 0)),
                  pl.BlockSpec((1, nl), lambda i: (0, 0))],
        out_specs=[pl.BlockSpec((tm, nl), lambda i: (i, 0))] * 2,
        out_shape=[jax.ShapeDtypeStruct((t, nl), jnp.int32), jax.ShapeDtypeStruct((t, nl), F32)],
        compiler_params=_cparams(("arbitrary",)),
        name="moe_router",
    )(h2, rw_pad, rb_pad)


def _row_copy(src_hbm, dst_vmem, sem, src_row, dst_row):
    return pltpu.make_async_copy(src_hbm.at[pl.ds(src_row, 1)], dst_vmem.at[pl.ds(dst_row, 1)], sem)


def _gather_kernel(tok_ref, src_ref, o_ref, buf, sem):
    nrow = buf.shape[0]

    def start(s, carry):
        _row_copy(src_ref, buf, sem, tok_ref[0, 0, s], s).start()
        return carry
    lax.fori_loop(0, nrow, start, 0)

    def wait(s, carry):
        _row_copy(src_ref, buf, sem, 0, s).wait()
        return carry
    lax.fori_loop(0, nrow, wait, 0)
    o_ref[...] = buf[...].astype(o_ref.dtype)


def _gather_rows(src, slot_tok, tm):
    _, d = src.shape
    n = slot_tok.shape[0]
    nb = n // tm
    return pl.pallas_call(
        _gather_kernel,
        grid=(nb,),
        in_specs=[pl.BlockSpec((1, 1, tm), lambda i: (i, 0, 0), memory_space=pltpu.SMEM),
                  pl.BlockSpec(memory_space=pl.ANY)],
        out_specs=pl.BlockSpec((tm, d), lambda i: (i, 0)),
        out_shape=jax.ShapeDtypeStruct((n, d), BF16),
        scratch_shapes=[pltpu.VMEM((tm, d), F32), pltpu.SemaphoreType.DMA(())],
        compiler_params=_cparams(("arbitrary",)),
        name="moe_gather",
    )(slot_tok.reshape(nb, 1, tm), src)


def _moe_up_kernel(be_ref, nv_ref, xs_ref, wg_ref, wu_ref, bg_ref, bu_ref, o_ref, wgb, wub):
    i = pl.program_id(1)
    e = be_ref[i]
    prev = be_ref[jnp.maximum(i - 1, 0)]

    @pl.when((i == 0) | (e != prev))
    def _():
        wgb[...] = wg_ref[...].astype(BF16)
        wub[...] = wu_ref[...].astype(BF16)

    @pl.when(i < nv_ref[0])
    def _():
        x = xs_ref[...]
        gate = _dot(x, wgb[...]) + bg_ref[0]
        up = _dot(x, wub[...]) + bu_ref[0]
        gate = jnp.minimum(gate, SWIGLU_LIMIT)
        up = jnp.clip(up, -SWIGLU_LIMIT, SWIGLU_LIMIT)
        act = (up + 1.0) * gate * jax.nn.sigmoid(SWIGLU_ALPHA * gate)
        o_ref[...] = act.astype(o_ref.dtype)

    @pl.when(i >= nv_ref[0])
    def _():
        o_ref[...] = jnp.zeros_like(o_ref)


def _moe_up(xs, w_gu, b_gu, block_e, nvalid, tm):
    n, d = xs.shape
    e, _, de2 = w_gu.shape
    de = de2 // 2
    tf = _pick_tile(de, (256, 128))
    nf = de // tf
    nb = n // tm
    grid_spec = pltpu.PrefetchScalarGridSpec(
        num_scalar_prefetch=2,
        grid=(nf, nb),
        in_specs=[pl.BlockSpec((tm, d), lambda j, i, be, nv: (i, 0)),
                  pl.BlockSpec((None, d, tf), lambda j, i, be, nv: (be[i], 0, j)),
                  pl.BlockSpec((None, d, tf), lambda j, i, be, nv: (be[i], 0, nf + j)),
                  pl.BlockSpec((None, 1, tf), lambda j, i, be, nv: (be[i], 0, j)),
                  pl.BlockSpec((None, 1, tf), lambda j, i, be, nv: (be[i], 0, nf + j))],
        out_specs=pl.BlockSpec((tm, tf), lambda j, i, be, nv: (i, j)),
        scratch_shapes=[pltpu.VMEM((d, tf), BF16), pltpu.VMEM((d, tf), BF16)],
    )
    b3 = b_gu.reshape(e, 1, de2)
    return pl.pallas_call(
        _moe_up_kernel,
        grid_spec=grid_spec,
        out_shape=jax.ShapeDtypeStruct((n, de), BF16),
        compiler_params=_cparams(("arbitrary", "arbitrary")),
        name="moe_gate_up",
    )(block_e, nvalid, xs, w_gu, w_gu, b3, b3)


def _moe_down_kernel(be_ref, nv_ref, a_ref, w_ref, b_ref, sw_ref, o_ref, wb):
    i = pl.program_id(1)
    e = be_ref[i]
    prev = be_ref[jnp.maximum(i - 1, 0)]

    @pl.when((i == 0) | (e != prev))
    def _():
        wb[...] = w_ref[...].astype(BF16)

    @pl.when(i < nv_ref[0])
    def _():
        o_ref[...] = (_dot(a_ref[...], wb[...]) + b_ref[0]) * sw_ref[...]

    @pl.when(i >= nv_ref[0])
    def _():
        o_ref[...] = jnp.zeros_like(o_ref)


def _moe_down(act, w_down, b_down, slot_w, block_e, nvalid, tm):
    n, de = act.shape
    e, _, d = w_down.shape
    tn = _pick_tile(d, (1024, 512, 256))
    nb = n // tm
    grid_spec = pltpu.PrefetchScalarGridSpec(
        num_scalar_prefetch=2,
        grid=(d // tn, nb),
        in_specs=[pl.BlockSpec((tm, de), lambda j, i, be, nv: (i, 0)),
                  pl.BlockSpec((None, de, tn), lambda j, i, be, nv: (be[i], 0, j)),
                  pl.BlockSpec((None, 1, tn), lambda j, i, be, nv: (be[i], 0, j)),
                  pl.BlockSpec((tm, 1), lambda j, i, be, nv: (i, 0))],
        out_specs=pl.BlockSpec((tm, tn), lambda j, i, be, nv: (i, j)),
        scratch_shapes=[pltpu.VMEM((de, tn), BF16)],
    )
    return pl.pallas_call(
        _moe_down_kernel,
        grid_spec=grid_spec,
        out_shape=jax.ShapeDtypeStruct((n, d), F32),
        compiler_params=_cparams(("arbitrary", "arbitrary")),
        name="moe_down",
    )(block_e, nvalid, act, w_down, b_down.reshape(e, 1, d), slot_w.reshape(n, 1))


def _combine_kernel(dest_ref, ys_ref, lat_ref, g2_ref, fg_ref, o_ref, buf, sem):
    tm = lat_ref.shape[1]
    nrow = TOP_K * tm

    def start(s, carry):
        _row_copy(ys_ref, buf, sem, dest_ref[0, 0, s], s).start()
        return carry
    lax.fori_loop(0, nrow, start, 0)

    def wait(s, carry):
        _row_copy(ys_ref, buf, sem, 0, s).wait()
        return carry
    lax.fori_loop(0, nrow, wait, 0)

    acc = buf[0:tm, :]
    for j in range(1, TOP_K):
        acc = acc + buf[j * tm:(j + 1) * tm, :]
    lat = lat_ref[0] + g2_ref[0] * acc
    ms = jnp.mean(lat * lat, axis=-1, keepdims=True)
    o_ref[0] = lat * lax.rsqrt(ms + EPS) * fg_ref[...]


def _combine(ys, dest, lat, g2, final_g):
    b, l, d = lat.shape
    tm = _pick_tile(l, (64,))
    nt = l // tm
    dest_blk = dest.reshape(b, nt, tm, TOP_K).transpose(0, 1, 3, 2).reshape(b * nt, 1, TOP_K * tm)
    return pl.pallas_call(
        _combine_kernel,
        grid=(b, nt),
        in_specs=[pl.BlockSpec((1, 1, TOP_K * tm), lambda bi, i: (bi * nt + i, 0, 0),
                               memory_space=pltpu.SMEM),
                  pl.BlockSpec(memory_space=pl.ANY),
                  pl.BlockSpec((1, tm, d), lambda bi, i: (bi, i, 0)),
                  pl.BlockSpec((1, 1, d), lambda bi, i: (bi, 0, 0)),
                  pl.BlockSpec((1, d), lambda bi, i: (0, 0))],
        out_specs=pl.BlockSpec((1, tm, d), lambda bi, i: (bi, i, 0)),
        out_shape=jax.ShapeDtypeStruct((b, l, d), F32),
        scratch_shapes=[pltpu.VMEM((TOP_K * tm, d), F32), pltpu.SemaphoreType.DMA(())],
        compiler_params=_cparams(("arbitrary", "arbitrary")),
        name="moe_combine_final_norm",
    )(dest_blk, ys, lat, g2, final_g)


def _route_slots(top_i, top_w, n_experts, tm):
    t = top_i.shape[0]
    n_assign = t * TOP_K
    flat_e = top_i.reshape(-1)
    onehot = (flat_e[:, None] == jnp.arange(n_experts, dtype=jnp.int32)[None, :]).astype(jnp.int32)
    csum = jnp.cumsum(onehot, axis=0)
    counts = csum[-1]
    rank = jnp.sum((csum - 1) * onehot, axis=1)
    padded = (counts + tm - 1) // tm * tm
    pad_end = jnp.cumsum(padded)
    start_pad = pad_end - padded
    dest = start_pad[flat_e] + rank
    n_blocks = n_assign // tm + n_experts
    n_slots = n_blocks * tm
    flat_t = jnp.arange(n_assign, dtype=jnp.int32) // TOP_K
    slot_tok = jnp.zeros((n_slots,), jnp.int32).at[dest].set(flat_t)
    slot_w = jnp.zeros((n_slots,), F32).at[dest].set(top_w.reshape(-1))
    block_start = jnp.arange(n_blocks, dtype=jnp.int32) * tm
    block_e = jnp.minimum(jnp.searchsorted(pad_end, block_start, side='right'),
                          n_experts - 1).astype(jnp.int32)
    nvalid = (pad_end[-1] // tm).astype(jnp.int32).reshape(1)
    return slot_tok, slot_w, dest.astype(jnp.int32), block_e, nvalid


def kernel(x, c, ctx, c_ctx, ada_w, ada_b, norm_mix_g, norm_ffn_g, w_in, ssd_conv_w, ssd_conv_b, ssd_dt_bias, ssd_a_log, ssd_d, ssd_norm_g, rw_mu, rw_w0, rw_w2, rw_a0, rw_a2, rw_k_k, rw_k_a, rw_r_k, rw_ln_g, rw_ln_b, rw_g2, w_br_ssd, w_br_rw, w_out, router_w, router_b, w_gu, b_gu, w_down, b_down, final_g):
    assert ada_w.shape[0] == 1, "single trunk layer"
    b, seq, d = x.shape
    cl = ctx.shape[1]
    t = cl + seq
    h = d // HEAD_DIM
    g = SSD_GROUPS
    r = h // g
    gn = g * SSD_STATE
    xbc = d + 2 * gn
    gate_rank = rw_g2.shape[1]
    n_experts = router_w.shape[2]
    assert d % 256 == 0 and r % 2 == 0 and seq % GRID_W == 0
    tb = _pick_tile(cl, (256, 128))
    assert cl % tb == 0 and seq % tb == 0 and tb % SSD_CHUNK == 0
    nctx = cl // tb

    cond = jnp.concatenate([jax.nn.silu(c), jax.nn.silu(c_ctx)[None, :]], axis=0)
    cond = jnp.pad(cond, ((0, 16 - (b + 1) % 16 if (b + 1) % 16 else 0), (0, 0))).astype(BF16)
    mod = _matmul(cond, ada_w[0], ada_b, F32, cond.shape[0], 512)
    sh1, sc1, g1, sh2, sc2, g2 = [mod[:, i * d:(i + 1) * d] for i in range(6)]
    lat3 = lambda m: m[:b, None, :]
    ctx3 = lambda m: m[b:b + 1, None, :]

    h_c = _norm_mod(ctx, norm_mix_g, ctx3(sh1), ctx3(sc1), BF16)
    h_l = _norm_mod(x, norm_mix_g, lat3(sh1), lat3(sc1), BF16)
    h_all = jnp.concatenate([h_c, h_l], axis=1).reshape(b * t, d)

    w = w_in[0]
    o_dt = d + xbc
    o_rw = o_dt + 2 * h
    o_low = o_rw + 3 * d
    n_low = 4 * LOW_RANK + gate_rank
    o_gate = o_low + n_low
    w_big = jnp.concatenate([w[:, :o_dt], w[:, o_rw:o_low], w[:, o_gate:]], axis=1).astype(BF16)
    low_pad = (-n_low) % 512
    w_small = jnp.concatenate(
        [w[:, o_low:o_gate], jnp.zeros((d, low_pad), F32),
         w[:, o_dt:o_rw], jnp.zeros((d, LOW_RANK - 2 * h), F32)], axis=1).astype(BF16)
    n_big = w_big.shape[1]
    n_small = w_small.shape[1]
    tm_in = _pick_tile(b * t, (1024, 512, 256, 128))
    p_big = _matmul(h_all, w_big, jnp.zeros((1, n_big), F32), BF16, tm_in, 512).reshape(b, t, n_big)
    p_small = _matmul(h_all, w_small, jnp.zeros((1, n_small), F32), F32, tm_in,
                      n_small).reshape(b, t, n_small)
    col_rkv = d + xbc
    col_gate = col_rkv + 3 * d

    xbc_act = _ssd_conv(p_big, ssd_conv_w[0], ssd_conv_b, d, xbc, cl)
    n_lowp = n_low + low_pad
    dt_pad = jnp.zeros((LOW_RANK - 2 * h,), F32)
    dtt = p_small[:, :, n_lowp:].reshape(b, t // SSD_CHUNK, SSD_CHUNK, LOW_RANK).transpose(0, 1, 3, 2)
    pc = jnp.stack([jnp.concatenate([ssd_dt_bias[0].reshape(-1), dt_pad]),
                    jnp.concatenate([-jnp.exp(ssd_a_log[0]).reshape(-1), dt_pad]),
                    jnp.concatenate([ssd_d[0].reshape(-1), dt_pad])], axis=0)
    pr = jnp.broadcast_to(pc[:2, :, None], (2, LOW_RANK, SSD_CHUNK))
    y_ssd = _ssd_scan(xbc_act, p_small, n_lowp, dtt, pc, pr, d, tb, nctx)
    ssd_o = _ssd_output(y_ssd, p_big, ssd_norm_g, d, cl, seq)

    mu = rw_mu[0]
    rkv = _token_shift(p_big, mu[None, :3 * d], col_rkv, 3 * d, cl, BF16)
    mu_low = jnp.pad(mu[3 * d:], (0, low_pad))[None, :]
    low = _token_shift(p_small, mu_low, 0, n_lowp, cl, F32)
    y_rw, bv = _rwkv_scan(rkv, low, rw_w0[0][:, None, :], rw_a0[0][:, None, :], rw_w2[0], rw_a2[0],
                          rw_k_k, rw_k_a, rw_r_k.reshape(1, d), d, tb, nctx)
    gr_pad = n_lowp - 4 * LOW_RANK
    g2p = jnp.pad(rw_g2[0], ((0, gr_pad - gate_rank), (0, 0)))
    rw_o = _rwkv_output(y_rw, bv, low, g2p, rw_ln_g, rw_ln_b, d, cl, seq)

    m = _merge(ssd_o, rw_o, w_br_ssd[0], w_br_rw[0], p_big, col_gate, cl)
    lat = _outproj(m, w_out[0], x, lat3(g1))

    h2 = _norm_mod(lat, norm_ffn_g, lat3(sh2), lat3(sc2), F32).reshape(b * seq, d)
    rw_pad = jnp.pad(router_w[0], ((0, 0), (0, 128 - n_experts))).astype(BF16)
    rb_pad = jnp.pad(router_b, ((0, 0), (0, 128 - n_experts)), constant_values=-1e30)
    idx, wts = _router(h2, rw_pad, rb_pad)
    slot_tok, slot_w, dest, block_e, nvalid = _route_slots(idx[:, :TOP_K], wts[:, :TOP_K],
                                                           n_experts, MOE_TM)
    xs = _gather_rows(h2, slot_tok, MOE_TM)
    act = _moe_up(xs, w_gu[0], b_gu[0], block_e, nvalid, MOE_TM)
    ys = _moe_down(act, w_down[0], b_down[0], slot_w, block_e, nvalid, MOE_TM)
    return _combine(ys, dest.reshape(b, seq, TOP_K), lat, lat3(g2), final_g[None, :])
```

```python
import functools

import jax
import jax.numpy as jnp
from jax import lax
from jax.experimental import pallas as pl
from jax.experimental.pallas import tpu as pltpu

F32 = jnp.float32
BF16 = jnp.bfloat16

HEAD_DIM = 64
SSD_GROUPS = 8
SSD_STATE = 128
SSD_CHUNK = 128
RW_CHUNK = 64
RW_LANES = 256
RW_GROUPS = 2
GRID_W = 64
LOW_RANK = 128
EPS = 1e-6
RWKV_GN_EPS = 64e-5
TOP_K = 4
SWIGLU_LIMIT = 7.0
SWIGLU_ALPHA = 1.702
MOE_TM = 256
VMEM_LIMIT_V7X = 56 * 1024 * 1024

NT_DIMS = (((1,), (1,)), ((), ()))
TN_DIMS = (((0,), (0,)), ((), ()))


def _cparams(sem):
    return pltpu.CompilerParams(dimension_semantics=sem, vmem_limit_bytes=VMEM_LIMIT_V7X)


def _dot(a, b):
    return jnp.dot(a, b, preferred_element_type=F32)


def _dotg(a, b, dims):
    return lax.dot_general(a, b, dims, preferred_element_type=F32)


def _split2(x):
    hi = x.astype(BF16)
    lo = (x - hi.astype(F32)).astype(BF16)
    return hi, lo


def _dot_x2(x, e):
    hi, lo = _split2(x)
    return _dot(hi, e) + _dot(lo, e)


def _dot_x3_left(m, x):
    hi = x.astype(BF16)
    r1 = x - hi.astype(F32)
    mid = r1.astype(BF16)
    lo = (r1 - mid.astype(F32)).astype(BF16)
    return _dot(m, hi) + _dot(m, mid) + _dot(m, lo)


def _dot_x3_right_nt(x, m):
    hi = x.astype(BF16)
    r1 = x - hi.astype(F32)
    mid = r1.astype(BF16)
    lo = (r1 - mid.astype(F32)).astype(BF16)
    return _dotg(hi, m, NT_DIMS) + _dotg(mid, m, NT_DIMS) + _dotg(lo, m, NT_DIMS)


def _softplus(x):
    return jnp.maximum(x, 0.0) + jnp.log1p(jnp.exp(-jnp.abs(x)))


def _silu(x):
    return x * jax.nn.sigmoid(x)


def _iota(shape, dim):
    return lax.broadcasted_iota(jnp.int32, shape, dim)


def _pick_tile(n, prefs):
    for p in prefs:
        if n % p == 0:
            return p
    return n


def _mm_kernel(a_ref, w_ref, bias_ref, o_ref, *scratch):
    if scratch:
        wb_ref, = scratch

        @pl.when(pl.program_id(1) == 0)
        def _():
            wb_ref[...] = w_ref[...].astype(BF16)
        w = wb_ref[...]
    else:
        w = w_ref[...]
    o_ref[...] = (_dot(a_ref[...], w) + bias_ref[...]).astype(o_ref.dtype)


def _matmul(a, w, bias, out_dtype, tm, tn):
    m, k = a.shape
    n = w.shape[1]
    scratch = [pltpu.VMEM((k, tn), BF16)] if w.dtype != BF16 else []
    return pl.pallas_call(
        _mm_kernel,
        grid=(n // tn, m // tm),
        in_specs=[pl.BlockSpec((tm, k), lambda j, i: (i, 0)),
                  pl.BlockSpec((k, tn), lambda j, i: (0, j)),
                  pl.BlockSpec((1, tn), lambda j, i: (0, j))],
        out_specs=pl.BlockSpec((tm, tn), lambda j, i: (i, j)),
        out_shape=jax.ShapeDtypeStruct((m, n), out_dtype),
        scratch_shapes=scratch,
        compiler_params=_cparams(("arbitrary", "arbitrary")),
        name="dense_matmul",
    )(a, w, bias)


def _norm_mod_kernel(x_ref, g_ref, sh_ref, sc_ref, o_ref):
    x = x_ref[0]
    ms = jnp.mean(x * x, axis=-1, keepdims=True)
    y = x * lax.rsqrt(ms + EPS) * g_ref[...]
    o_ref[0] = (y * (1.0 + sc_ref[0]) + sh_ref[0]).astype(o_ref.dtype)


def _norm_mod(x, g, shift, scale, out_dtype):
    b, l, d = x.shape
    tl = _pick_tile(l, (256, 128, 64))
    per_batch = shift.shape[0] == b
    mod_map = (lambda bi, i: (bi, 0, 0)) if per_batch else (lambda bi, i: (0, 0, 0))
    return pl.pallas_call(
        _norm_mod_kernel,
        grid=(b, l // tl),
        in_specs=[pl.BlockSpec((1, tl, d), lambda bi, i: (bi, i, 0)),
                  pl.BlockSpec((1, d), lambda bi, i: (0, 0)),
                  pl.BlockSpec((1, 1, d), mod_map),
                  pl.BlockSpec((1, 1, d), mod_map)],
        out_specs=pl.BlockSpec((1, tl, d), lambda bi, i: (bi, i, 0)),
        out_shape=jax.ShapeDtypeStruct((b, l, d), out_dtype),
        compiler_params=_cparams(("arbitrary", "arbitrary")),
        name="norm_modulate",
    )(x, g, shift, scale)


def _conv_segment(u, w, bias):
    n = u.shape[0]
    row = _iota(u.shape, 0)
    prev = jnp.where(row == 0, 0.0, pltpu.roll(u, 1, 0))
    nxt = jnp.where(row == n - 1, 0.0, pltpu.roll(u, n - 1, 0))
    y = w[0:1] * prev + w[1:2] * u + w[2:3] * nxt + bias
    return _silu(y)


def _conv_kernel(u_ref, w_ref, b_ref, o_ref, *, ctx):
    w = w_ref[...]
    bias = b_ref[...]
    t = u_ref.shape[1]
    o_ref[0, 0:ctx, :] = _conv_segment(u_ref[0, 0:ctx, :].astype(F32), w, bias).astype(o_ref.dtype)
    o_ref[0, ctx:t, :] = _conv_segment(u_ref[0, ctx:t, :].astype(F32), w, bias).astype(o_ref.dtype)


def _ssd_conv(p_big, conv_w, conv_b, d, xbc, ctx):
    b, t, _ = p_big.shape
    ct = 256
    off = d // ct
    return pl.pallas_call(
        functools.partial(_conv_kernel, ctx=ctx),
        grid=(b, xbc // ct),
        in_specs=[pl.BlockSpec((1, t, ct), lambda bi, j: (bi, 0, off + j)),
                  pl.BlockSpec((3, ct), lambda bi, j: (0, j)),
                  pl.BlockSpec((1, ct), lambda bi, j: (0, j))],
        out_specs=pl.BlockSpec((1, t, ct), lambda bi, j: (bi, 0, j)),
        out_shape=jax.ShapeDtypeStruct((b, t, xbc), BF16),
        compiler_params=_cparams(("arbitrary", "arbitrary")),
        name="ssd_conv_silu",
    )(p_big, conv_w, conv_b)


def _ssd_kernel(x_ref, b_ref, c_ref, dt_ref, dtt_ref, pc_ref, pr_ref, y_ref, st_ref, *, nq, r, h):
    q = SSD_CHUNK
    rp = r * HEAD_DIM
    nl = dt_ref.shape[2]
    gi = pl.program_id(1)
    d = pl.program_id(2)
    s = pl.program_id(3)
    fwd = d == 0
    base = d * h + gi * r

    @pl.when(s == 0)
    def _():
        st_ref[...] = jnp.zeros_like(st_ref)

    rowi = _iota((q, q), 0)
    coli = _iota((q, q), 1)
    sgn = 1 - 2 * d
    incl = (rowi - coli) * sgn >= 0
    tri = incl.astype(BF16)
    expand = (_iota((nl, rp), 0) == base + _iota((nl, rp), 1) // HEAD_DIM).astype(BF16)
    lane_head = _iota((1, 2 * HEAD_DIM), 1) // HEAD_DIM
    lane_c = _iota((q, nl), 1)
    row_r = _iota((nl, q), 0)

    bias_c, a_c, dsk_c = pc_ref[0:1, :], pc_ref[1:2, :], pc_ref[2:3, :]
    bias_r, a_r = pr_ref[0], pr_ref[1]
    dsk_x = _dot_x2(jnp.broadcast_to(dsk_c, (8, nl)), expand)[0:1]

    for j in range(nq):
        jj = jnp.where(fwd, j, nq - 1 - j)
        off = pl.multiple_of(jj * q, q)
        x = x_ref[0, pl.ds(off, q), :].astype(F32)
        bc = b_ref[0, pl.ds(off, q), :]
        cc = c_ref[0, pl.ds(off, q), :]
        dt_c = _softplus(dt_ref[0, pl.ds(off, q), :] + bias_c)
        dt_r = _softplus(dtt_ref[0, jj] + bias_r)
        al_c = dt_c * a_c
        al_r = dt_r * a_r
        acs_c = _dot_x3_left(tri, al_c)
        acs_r = _dot_x3_right_nt(al_r, tri)
        tot_c = jnp.sum(al_c, axis=0, keepdims=True)

        xdt = x * _dot_x2(dt_c, expand)
        cb = _dotg(cc, bc, NT_DIMS)
        pieces = []
        for pair in range(r // 2):
            xp = xdt[:, pair * 2 * HEAD_DIM:(pair + 1) * 2 * HEAD_DIM]
            acc = None
            for hh in range(2):
                hcol = base + pair * 2 + hh
                col = jnp.sum(jnp.where(lane_c == hcol, acs_c, 0.0), axis=1, keepdims=True)
                row = jnp.sum(jnp.where(row_r == hcol, acs_r, 0.0), axis=0, keepdims=True)
                lmat = jnp.exp(jnp.where(incl, col - row, -jnp.inf))
                m = (cb * lmat).astype(BF16)
                xm = jnp.where(lane_head == hh, xp, 0.0).astype(BF16)
                term = _dot(m, xm)
                acc = term if acc is None else acc + term
            pieces.append(acc)
        y_diag = pieces[0] if len(pieces) == 1 else jnp.concatenate(pieces, axis=1)

        st = st_ref[...]
        y_off = _dot(cc, st.astype(BF16)) * _dot_x2(jnp.exp(acs_c), expand)
        y = y_diag + y_off + x * dsk_x
        y_ref[0, 0, pl.ds(off, q), :] = y.astype(y_ref.dtype)

        dec = _dot_x2(jnp.exp(tot_c - acs_c), expand)
        upd = _dotg(bc, (xdt * dec).astype(BF16), TN_DIMS)
        end = _dot_x2(jnp.broadcast_to(jnp.exp(tot_c), (8, nl)), expand)[0:1]
        st_ref[...] = st * end + upd


def _block_order(s, d, nctx, nblk):
    bwd = jnp.where(s < nctx, nctx - 1 - s, nblk - 1 - (s - nctx))
    return jnp.where(d == 0, s, bwd)


def _ssd_scan(xbc_act, p_small, dt_col0, dtt, pc, pr, d, tb, nctx):
    b, t, _ = xbc_act.shape
    g = SSD_GROUPS
    h = d // HEAD_DIM
    r = h // g
    rp = r * HEAD_DIM
    nblk = t // tb
    nq = tb // SSD_CHUNK
    n = SSD_STATE
    nl = pc.shape[1]
    blk = lambda s, dd: _block_order(s, dd, nctx, nblk)
    return pl.pallas_call(
        functools.partial(_ssd_kernel, nq=nq, r=r, h=h),
        grid=(b, g, 2, nblk),
        in_specs=[
            pl.BlockSpec((1, tb, rp), lambda bi, gi, dd, s: (bi, blk(s, dd), gi)),
            pl.BlockSpec((1, tb, n), lambda bi, gi, dd, s: (bi, blk(s, dd), d // n + gi)),
            pl.BlockSpec((1, tb, n), lambda bi, gi, dd, s: (bi, blk(s, dd), d // n + g + gi)),
            pl.BlockSpec((1, tb, nl), lambda bi, gi, dd, s: (bi, blk(s, dd), dt_col0 // nl)),
            pl.BlockSpec((1, nq, nl, SSD_CHUNK), lambda bi, gi, dd, s: (bi, blk(s, dd), 0, 0)),
            pl.BlockSpec((3, nl), lambda bi, gi, dd, s: (0, 0)),
            pl.BlockSpec((2, nl, SSD_CHUNK), lambda bi, gi, dd, s: (0, 0, 0)),
        ],
        out_specs=pl.BlockSpec((1, 1, tb, rp), lambda bi, gi, dd, s: (dd, bi, blk(s, dd), gi)),
        out_shape=jax.ShapeDtypeStruct((2, b, t, d), BF16),
        scratch_shapes=[pltpu.VMEM((n, rp), F32)],
        compiler_params=_cparams(("arbitrary",) * 4),
        name="ssd_scan",
    )(xbc_act, xbc_act, xbc_act, p_small, dtt, pc, pr)


def _ssd_out_kernel(y_ref, z_ref, g_ref, o_ref):
    y = y_ref[0, 0].astype(F32) + y_ref[1, 0].astype(F32)
    u = y * _silu(z_ref[0].astype(F32))
    u = u * lax.rsqrt(jnp.mean(u * u, axis=-1, keepdims=True) + EPS)
    o_ref[0] = (u * g_ref[...]).astype(o_ref.dtype)


def _ssd_output(y_ssd, p_big, norm_g, d, ctx, seq):
    _, b, t, _ = y_ssd.shape
    gw = d // SSD_GROUPS
    tl = _pick_tile(seq, (256, 128, 64))
    tl = tl if ctx % tl == 0 else _pick_tile(ctx, (128, 64))
    o = ctx // tl
    return pl.pallas_call(
        _ssd_out_kernel,
        grid=(b, seq // tl, SSD_GROUPS),
        in_specs=[pl.BlockSpec((2, 1, tl, gw), lambda bi, i, gi: (0, bi, o + i, gi)),
                  pl.BlockSpec((1, tl, gw), lambda bi, i, gi: (bi, o + i, gi)),
                  pl.BlockSpec((1, gw), lambda bi, i, gi: (0, gi))],
        out_specs=pl.BlockSpec((1, tl, gw), lambda bi, i, gi: (bi, i, gi)),
        out_shape=jax.ShapeDtypeStruct((b, seq, d), BF16),
        compiler_params=_cparams(("arbitrary",) * 3),
        name="ssd_gated_norm",
    )(y_ssd, p_big, norm_g)


def _shift_kernel(u_ref, mu_ref, o_ref, *, ctx):
    t = u_ref.shape[1]
    mu = mu_ref[...]
    uc = u_ref[0, 0:ctx, :].astype(F32)
    row = _iota(uc.shape, 0)
    lane = _iota(uc.shape, 1)
    prev = jnp.where(row == 0, 0.0, pltpu.roll(uc, 1, 0))
    nxt = jnp.where(row == ctx - 1, 0.0, pltpu.roll(uc, ctx - 1, 0))
    sh = jnp.where(lane % 2 == 0, prev, nxt)
    o_ref[0, 0:ctx, :] = (uc + mu * (sh - uc)).astype(o_ref.dtype)
    ul = u_ref[0, ctx:t, :].astype(F32)
    n = t - ctx
    row = _iota(ul.shape, 0)
    lane = _iota(ul.shape, 1)
    col = row % GRID_W
    left = jnp.where(col == 0, 0.0, pltpu.roll(ul, 1, 0))
    right = jnp.where(col == GRID_W - 1, 0.0, pltpu.roll(ul, n - 1, 0))
    up = jnp.where(row < GRID_W, 0.0, pltpu.roll(ul, GRID_W, 0))
    down = jnp.where(row >= n - GRID_W, 0.0, pltpu.roll(ul, n - GRID_W, 0))
    c4 = lane % 4
    sh = jnp.where(c4 == 0, left, jnp.where(c4 == 1, right, jnp.where(c4 == 2, up, down)))
    o_ref[0, ctx:t, :] = (ul + mu * (sh - ul)).astype(o_ref.dtype)


def _token_shift(p, mu, col0, width, ctx, out_dtype):
    b, t, _ = p.shape
    ct = 256
    off = col0 // ct
    return pl.pallas_call(
        functools.partial(_shift_kernel, ctx=ctx),
        grid=(b, width // ct),
        in_specs=[pl.BlockSpec((1, t, ct), lambda bi, j: (bi, 0, off + j)),
                  pl.BlockSpec((1, ct), lambda bi, j: (0, j))],
        out_specs=pl.BlockSpec((1, t, ct), lambda bi, j: (bi, 0, j)),
        out_shape=jax.ShapeDtypeStruct((b, t, width), out_dtype),
        compiler_params=_cparams(("arbitrary", "arbitrary")),
        name="rwkv_token_shift",
    )(p, mu)


def _rwkv_kernel(r_ref, k_ref, v_ref, wlo_ref, alo_ref, w0_ref, a0_ref, w2_ref, a2_ref,
                 kk_ref, ka_ref, rk_ref, y_ref, bv_ref, s_ref, *, nch, ng):
    c = RW_CHUNK
    w = RW_LANES
    hb = w // HEAD_DIM
    n = hb * c
    assert n == w and c == HEAD_DIM
    d = pl.program_id(1)
    s = pl.program_id(3)
    sgn = 1 - 2 * d

    @pl.when(s == 0)
    def _():
        s_ref[...] = jnp.zeros_like(s_ref)

    bdm = (_iota((n, w), 0) // c) == (_iota((n, w), 1) // HEAD_DIM)
    rel = (_iota((c, n), 0) - _iota((c, n), 1) % c) * sgn
    strict = rel > 0
    incl = rel >= 0
    eye = jnp.where(rel == 0, 1.0, 0.0)
    tri = jnp.where((_iota((c, c), 0) - _iota((c, c), 1)) * sgn >= 0, 1.0, 0.0).astype(BF16)
    eblk = jnp.where(bdm, 1.0, 0.0).astype(BF16)

    def expand(x):
        return jnp.where(bdm, jnp.concatenate([x] * hb, axis=0), 0.0).astype(BF16)

    chains = [(j, g) for j in range(nch) for g in range(ng)]
    offs = [pl.multiple_of((j + d * (nch - 1 - 2 * j)) * c, c) for j in range(nch)]
    lanes = [slice(g * w, (g + 1) * w) for g in range(ng)]
    tw = [jnp.tanh(wlo_ref[0, pl.ds(offs[j], c), :]).astype(BF16) for j in range(nch)]
    al = [alo_ref[0, pl.ds(offs[j], c), :].astype(BF16) for j in range(nch)]
    ld = lambda ref, j, g: ref[0, pl.ds(offs[j], c), lanes[g]].astype(F32)
    r = [ld(r_ref, j, g) for j, g in chains]
    k = [ld(k_ref, j, g) for j, g in chains]
    v = [ld(v_ref, j, g) for j, g in chains]
    nc = len(chains)
    ids = range(nc)

    wl = [w0_ref[0, :, lanes[g]] + _dot(tw[j], w2_ref[0, :, lanes[g]].astype(BF16)) for j, g in chains]
    ai = [a0_ref[0, :, lanes[g]] + _dot(al[j], a2_ref[0, :, lanes[g]].astype(BF16)) for j, g in chains]
    logw = [-jnp.exp(-_softplus(-x) - 0.5) for x in wl]
    asig = [jax.nn.sigmoid(x) for x in ai]
    kkr = [k[i] * kk_ref[:, lanes[chains[i][1]]] for i in ids]
    kd = [k[i] * (1.0 + (asig[i] - 1.0) * ka_ref[:, lanes[chains[i][1]]]) for i in ids]
    sums = [_dot_x2(jnp.concatenate([kkr[i] * kkr[i], r[i] * kd[i] * rk_ref[:, lanes[chains[i][1]]]], axis=0),
                    eblk) for i in ids]
    cs = [_dot_x3_left(tri, x) for x in logw]
    for i, (j, g) in enumerate(chains):
        bv_ref[0, 0, pl.ds(offs[j], c), lanes[g]] = (sums[i][c:2 * c] * v[i]).astype(bv_ref.dtype)
    kk = [kkr[i] / jnp.maximum(jnp.sqrt(sums[i][0:c]), 1e-12) for i in ids]
    kb = [kk[i] * asig[i] for i in ids]
    tot = [jnp.sum(x, axis=0, keepdims=True) for x in logw]
    e_neg = [jnp.exp(-x) for x in cs]
    at = [-kk[i] * jnp.exp(cs[i] - logw[i]) for i in ids]
    rt = [r[i] * jnp.exp(cs[i]) for i in ids]
    ar = [jnp.concatenate([at[i], rt[i]], axis=0).astype(BF16) for i in ids]
    xbk = [jnp.concatenate([expand(kb[i] * e_neg[i]), expand(kd[i] * e_neg[i])], axis=0) for i in ids]
    p = [_dotg(ar[i], xbk[i], NT_DIMS) for i in ids]
    nab = [jnp.where(strict, x[0:c, 0:n], 0.0) for x in p]
    aak = [jnp.where(strict, x[0:c, n:2 * n], 0.0).astype(BF16) for x in p]
    arb = [jnp.where(incl, x[c:2 * c, 0:n], 0.0).astype(BF16) for x in p]
    ark = [jnp.where(incl, x[c:2 * c, n:2 * n], 0.0).astype(BF16) for x in p]
    vbd = [expand(x) for x in v]
    av = [_dot(aak[i], vbd[i]) for i in ids]
    yv = [_dot(ark[i], vbd[i]) for i in ids]

    tinv = [eye + x for x in nab]
    npow = nab
    for _ in range(c.bit_length() - 2):
        npow = [_dot(x.astype(BF16), expand(x)) for x in npow]
        tinv = [tinv[i] + _dot(tinv[i].astype(BF16), expand(npow[i])) for i in ids]

    tau = [_dot(tinv[i].astype(BF16), jnp.concatenate([expand(at[i]), expand(av[i])], axis=1)) for i in ids]
    ahat = [x[:, 0:w] for x in tau]
    u0 = [x[:, w:2 * w] for x in tau]
    yq = [_dot(arb[i], jnp.concatenate([expand(ahat[i]), expand(u0[i])], axis=1)) for i in ids]
    q = [(rt[i] + yq[i][:, 0:w]).astype(BF16) for i in ids]
    y0 = [yq[i][:, w:2 * w] + yv[i] for i in ids]
    e_end = [jnp.exp(tot[i] - cs[i]) for i in ids]
    bh = [(kb[i] * e_end[i]).astype(BF16) for i in ids]
    kh = [(kd[i] * e_end[i]).astype(BF16) for i in ids]
    m1 = [jnp.where(bdm, _dotg(ahat[i].astype(BF16), bh[i], TN_DIMS), 0.0).astype(BF16) for i in ids]
    m2 = [jnp.where(bdm, _dotg(jnp.concatenate([u0[i], v[i]], axis=0).astype(BF16),
                               jnp.concatenate([bh[i], kh[i]], axis=0), TN_DIMS), 0.0) for i in ids]
    gam = [jnp.exp(x) for x in tot]

    st = [s_ref[g] for g in range(ng)]
    for j in range(nch):
        for g in range(ng):
            i = j * ng + g
            sb = st[g].astype(BF16)
            y = _dotg(q[i], sb, NT_DIMS) + y0[i]
            y_ref[0, 0, pl.ds(offs[j], c), lanes[g]] = y.astype(y_ref.dtype)
            st[g] = st[g] * gam[i] + _dot(sb, m1[i]) + m2[i]
    for g in range(ng):
        s_ref[g] = st[g]


def _rwkv_scan(rkv, low, w0, a0, w2, a2, k_k, k_a, r_k, d, tb, nctx):
    b, t, _ = rkv.shape
    ng = RW_GROUPS if (d // RW_LANES) % RW_GROUPS == 0 else 1
    w = RW_LANES * ng
    hg = d // w
    nblk = t // tb
    nch = tb // RW_CHUNK
    dw = d // w
    blk = lambda s, dd: _block_order(s, dd, nctx, nblk)
    tok = lambda cb: pl.BlockSpec((1, tb, w), lambda bi, dd, hi, s: (bi, blk(s, dd), cb + hi))
    par = lambda: pl.BlockSpec((1, w), lambda bi, dd, hi, s: (0, hi))
    pard = lambda: pl.BlockSpec((1, 1, w), lambda bi, dd, hi, s: (dd, 0, hi))
    lowrank = lambda: pl.BlockSpec((1, LOW_RANK, w), lambda bi, dd, hi, s: (dd, 0, hi))
    out = pl.BlockSpec((1, 1, tb, w), lambda bi, dd, hi, s: (dd, bi, blk(s, dd), hi))
    return pl.pallas_call(
        functools.partial(_rwkv_kernel, nch=nch, ng=ng),
        grid=(b, 2, hg, nblk),
        in_specs=[tok(0), tok(dw), tok(2 * dw),
                  pl.BlockSpec((1, tb, LOW_RANK), lambda bi, dd, hi, s: (bi, blk(s, dd), dd)),
                  pl.BlockSpec((1, tb, LOW_RANK), lambda bi, dd, hi, s: (bi, blk(s, dd), 2 + dd)),
                  pard(), pard(), lowrank(), lowrank(), par(), par(), par()],
        out_specs=[out, out],
        out_shape=[jax.ShapeDtypeStruct((2, b, t, d), BF16)] * 2,
        scratch_shapes=[pltpu.VMEM((ng, RW_LANES, RW_LANES), F32)],
        compiler_params=_cparams(("arbitrary",) * 4),
        name="rwkv7_scan",
    )(rkv, rkv, rkv, low, low, w0, a0, w2, a2, k_k, k_a, r_k)


def _rwkv_out_kernel(y_ref, bv_ref, glo_ref, g2_ref, lng_ref, lnb_ref, o_ref):
    tn = o_ref.shape[2]
    eblk = ((_iota((tn, tn), 0) // HEAD_DIM) == (_iota((tn, tn), 1) // HEAD_DIM)).astype(BF16)
    y = y_ref[0, 0].astype(F32) + y_ref[1, 0].astype(F32)
    mu = _dot_x2(y, eblk) * (1.0 / HEAD_DIM)
    yc = y - mu
    var = _dot_x2(yc * yc, eblk) * (1.0 / HEAD_DIM)
    yn = yc * lax.rsqrt(var + RWKV_GN_EPS) * lng_ref[...] + lnb_ref[...]
    g = _dot(jax.nn.sigmoid(glo_ref[0]).astype(BF16), g2_ref[...].astype(BF16))
    bonus = bv_ref[0, 0].astype(F32) + bv_ref[1, 0].astype(F32)
    o_ref[0] = ((yn + bonus) * g).astype(o_ref.dtype)


def _rwkv_output(y_rw, bv, low, g2p, ln_g, ln_b, d, ctx, seq):
    _, b, t, _ = y_rw.shape
    tn = 256
    tl = _pick_tile(seq, (256, 128, 64))
    tl = tl if ctx % tl == 0 else _pick_tile(ctx, (128, 64))
    o = ctx // tl
    gr = g2p.shape[0]
    return pl.pallas_call(
        _rwkv_out_kernel,
        grid=(b, seq // tl, d // tn),
        in_specs=[pl.BlockSpec((2, 1, tl, tn), lambda bi, i, j: (0, bi, o + i, j)),
                  pl.BlockSpec((2, 1, tl, tn), lambda bi, i, j: (0, bi, o + i, j)),
                  pl.BlockSpec((1, tl, gr), lambda bi, i, j: (bi, o + i, 1)),
                  pl.BlockSpec((gr, tn), lambda bi, i, j: (0, j)),
                  pl.BlockSpec((1, tn), lambda bi, i, j: (0, j)),
                  pl.BlockSpec((1, tn), lambda bi, i, j: (0, j))],
        out_specs=pl.BlockSpec((1, tl, tn), lambda bi, i, j: (bi, i, j)),
        out_shape=jax.ShapeDtypeStruct((b, seq, d), BF16),
        compiler_params=_cparams(("arbitrary",) * 3),
        name="rwkv_groupnorm_gate",
    )(y_rw, bv, low, g2p, ln_g, ln_b)


def _merge_kernel(a1_ref, a2_ref, w1_ref, w2_ref, gs_ref, gr_ref, o_ref, w1b, w2b):
    @pl.when((pl.program_id(1) == 0) & (pl.program_id(2) == 0))
    def _():
        w1b[...] = w1_ref[...].astype(BF16)
        w2b[...] = w2_ref[...].astype(BF16)
    gs = jax.nn.sigmoid(gs_ref[0].astype(F32))
    gr = jax.nn.sigmoid(gr_ref[0].astype(F32))
    m = gs * _dot(a1_ref[0], w1b[...]) + gr * _dot(a2_ref[0], w2b[...])
    o_ref[0] = m.astype(o_ref.dtype)


def _merge(a1, a2, w1, w2, p_big, gate_col0, ctx):
    b, l, d = a1.shape
    tn = 512
    tm = _pick_tile(l, (256, 128, 64))
    tm = tm if ctx % tm == 0 else _pick_tile(ctx, (128, 64))
    o = ctx // tm
    gs0 = gate_col0 // tn
    gr0 = (gate_col0 + d) // tn
    return pl.pallas_call(
        _merge_kernel,
        grid=(d // tn, b, l // tm),
        in_specs=[pl.BlockSpec((1, tm, d), lambda j, bi, i: (bi, i, 0)),
                  pl.BlockSpec((1, tm, d), lambda j, bi, i: (bi, i, 0)),
                  pl.BlockSpec((d, tn), lambda j, bi, i: (0, j)),
                  pl.BlockSpec((d, tn), lambda j, bi, i: (0, j)),
                  pl.BlockSpec((1, tm, tn), lambda j, bi, i: (bi, o + i, gs0 + j)),
                  pl.BlockSpec((1, tm, tn), lambda j, bi, i: (bi, o + i, gr0 + j))],
        out_specs=pl.BlockSpec((1, tm, tn), lambda j, bi, i: (bi, i, j)),
        out_shape=jax.ShapeDtypeStruct((b, l, d), BF16),
        scratch_shapes=[pltpu.VMEM((d, tn), BF16), pltpu.VMEM((d, tn), BF16)],
        compiler_params=_cparams(("arbitrary",) * 3),
        name="branch_merge",
    )(a1, a2, w1, w2, p_big, p_big)


def _outproj_kernel(m_ref, w_ref, x_ref, g_ref, o_ref, wb):
    @pl.when((pl.program_id(1) == 0) & (pl.program_id(2) == 0))
    def _():
        wb[...] = w_ref[...].astype(BF16)
    o_ref[0] = x_ref[0] + g_ref[0] * _dot(m_ref[0], wb[...])


def _outproj(m, w, x, gate):
    b, l, d = m.shape
    tn = 512
    tm = _pick_tile(l, (512, 256, 128, 64))
    return pl.pallas_call(
        _outproj_kernel,
        grid=(d // tn, b, l // tm),
        in_specs=[pl.BlockSpec((1, tm, d), lambda j, bi, i: (bi, i, 0)),
                  pl.BlockSpec((d, tn), lambda j, bi, i: (0, j)),
                  pl.BlockSpec((1, tm, tn), lambda j, bi, i: (bi, i, j)),
                  pl.BlockSpec((1, 1, tn), lambda j, bi, i: (bi, 0, j))],
        out_specs=pl.BlockSpec((1, tm, tn), lambda j, bi, i: (bi, i, j)),
        out_shape=jax.ShapeDtypeStruct((b, l, d), F32),
        scratch_shapes=[pltpu.VMEM((d, tn), BF16)],
        compiler_params=_cparams(("arbitrary",) * 3),
        name="out_proj_residual",
    )(m, w, x, gate)


def _router_kernel(h_ref, w_ref, b_ref, idx_ref, wt_ref):
    logits = _dot(h_ref[...].astype(BF16), w_ref[...]) + b_ref[...]
    lane = _iota(logits.shape, 1)
    nl = logits.shape[1]
    lane_f = lane.astype(F32)
    work = logits
    idx_out = jnp.zeros(logits.shape, jnp.int32)
    val_out = jnp.zeros(logits.shape, F32)
    v0 = None
    for j in range(TOP_K):
        m = jnp.max(work, axis=-1, keepdims=True)
        sel = jnp.min(jnp.where(work == m, lane_f, float(nl)), axis=-1,
                      keepdims=True).astype(jnp.int32)
        v0 = m if v0 is None else v0
        idx_out = jnp.where(lane == j, sel, idx_out)
        val_out = jnp.where(lane == j, jnp.exp(m - v0), val_out)
        work = jnp.where(lane == sel, -jnp.inf, work)
    idx_ref[...] = idx_out
    wt_ref[...] = val_out / jnp.sum(val_out, axis=-1, keepdims=True)


def _router(h2, rw_pad, rb_pad):
    t, d = h2.shape
    tm = _pick_tile(t, (512, 256, 128))
    nl = rw_pad.shape[1]
    return pl.pallas_call(
        _router_kernel,
        grid=(t // tm,),
        in_specs=[pl.BlockSpec((tm, d), lambda i: (i, 0)),
                  pl.BlockSpec((d, nl), lambda i: (0, 0)),
                  pl.BlockSpec((1, nl), lambda i: (0, 0))],
        out_specs=[pl.BlockSpec((tm, nl), lambda i: (i, 0))] * 2,
        out_shape=[jax.ShapeDtypeStruct((t, nl), jnp.int32), jax.ShapeDtypeStruct((t, nl), F32)],
        compiler_params=_cparams(("arbitrary",)),
        name="moe_router",
    )(h2, rw_pad, rb_pad)


def _row_copy(src_hbm, dst_vmem, sem, src_row, dst_row):
    return pltpu.make_async_copy(src_hbm.at[pl.ds(src_row, 1)], dst_vmem.at[pl.ds(dst_row, 1)], sem)


def _gather_kernel(tok_ref, src_ref, o_ref, buf, sem):
    nrow = buf.shape[0]

    def start(s, carry):
        _row_copy(src_ref, buf, sem, tok_ref[0, 0, s], s).start()
        return carry
    lax.fori_loop(0, nrow, start, 0)

    def wait(s, carry):
        _row_copy(src_ref, buf, sem, 0, s).wait()
        return carry
    lax.fori_loop(0, nrow, wait, 0)
    o_ref[...] = buf[...].astype(o_ref.dtype)


def _gather_rows(src, slot_tok, tm):
    _, d = src.shape
    n = slot_tok.shape[0]
    nb = n // tm
    return pl.pallas_call(
        _gather_kernel,
        grid=(nb,),
        in_specs=[pl.BlockSpec((1, 1, tm), lambda i: (i, 0, 0), memory_space=pltpu.SMEM),
                  pl.BlockSpec(memory_space=pl.ANY)],
        out_specs=pl.BlockSpec((tm, d), lambda i: (i, 0)),
        out_shape=jax.ShapeDtypeStruct((n, d), BF16),
        scratch_shapes=[pltpu.VMEM((tm, d), F32), pltpu.SemaphoreType.DMA(())],
        compiler_params=_cparams(("arbitrary",)),
        name="moe_gather",
    )(slot_tok.reshape(nb, 1, tm), src)


def _moe_up_kernel(be_ref, nv_ref, xs_ref, wg_ref, wu_ref, bg_ref, bu_ref, o_ref, wgb, wub):
    i = pl.program_id(1)
    e = be_ref[i]
    prev = be_ref[jnp.maximum(i - 1, 0)]

    @pl.when((i == 0) | (e != prev))
    def _():
        wgb[...] = wg_ref[...].astype(BF16)
        wub[...] = wu_ref[...].astype(BF16)

    @pl.when(i < nv_ref[0])
    def _():
        x = xs_ref[...]
        gate = _dot(x, wgb[...]) + bg_ref[0]
        up = _dot(x, wub[...]) + bu_ref[0]
        gate = jnp.minimum(gate, SWIGLU_LIMIT)
        up = jnp.clip(up, -SWIGLU_LIMIT, SWIGLU_LIMIT)
        act = (up + 1.0) * gate * jax.nn.sigmoid(SWIGLU_ALPHA * gate)
        o_ref[...] = act.astype(o_ref.dtype)

    @pl.when(i >= nv_ref[0])
    def _():
        o_ref[...] = jnp.zeros_like(o_ref)


def _moe_up(xs, w_gu, b_gu, block_e, nvalid, tm):
    n, d = xs.shape
    e, _, de2 = w_gu.shape
    de = de2 // 2
    tf = _pick_tile(de, (512, 256, 128))
    nf = de // tf
    nb = n // tm
    grid_spec = pltpu.PrefetchScalarGridSpec(
        num_scalar_prefetch=2,
        grid=(nf, nb),
        in_specs=[pl.BlockSpec((tm, d), lambda j, i, be, nv: (i, 0)),
                  pl.BlockSpec((None, d, tf), lambda j, i, be, nv: (be[i], 0, j)),
                  pl.BlockSpec((None, d, tf), lambda j, i, be, nv: (be[i], 0, nf + j)),
                  pl.BlockSpec((None, 1, tf), lambda j, i, be, nv: (be[i], 0, j)),
                  pl.BlockSpec((None, 1, tf), lambda j, i, be, nv: (be[i], 0, nf + j))],
        out_specs=pl.BlockSpec((tm, tf), lambda j, i, be, nv: (i, j)),
        scratch_shapes=[pltpu.VMEM((d, tf), BF16), pltpu.VMEM((d, tf), BF16)],
    )
    b3 = b_gu.reshape(e, 1, de2)
    return pl.pallas_call(
        _moe_up_kernel,
        grid_spec=grid_spec,
        out_shape=jax.ShapeDtypeStruct((n, de), BF16),
        compiler_params=_cparams(("arbitrary", "arbitrary")),
        name="moe_gate_up",
    )(block_e, nvalid, xs, w_gu, w_gu, b3, b3)


def _moe_down_kernel(be_ref, nv_ref, a_ref, w_ref, b_ref, sw_ref, o_ref, wb):
    i = pl.program_id(1)
    e = be_ref[i]
    prev = be_ref[jnp.maximum(i - 1, 0)]

    @pl.when((i == 0) | (e != prev))
    def _():
        wb[...] = w_ref[...].astype(BF16)

    @pl.when(i < nv_ref[0])
    def _():
        o_ref[...] = (_dot(a_ref[...], wb[...]) + b_ref[0]) * sw_ref[...]

    @pl.when(i >= nv_ref[0])
    def _():
        o_ref[...] = jnp.zeros_like(o_ref)


def _moe_down(act, w_down, b_down, slot_w, block_e, nvalid, tm):
    n, de = act.shape
    e, _, d = w_down.shape
    tn = _pick_tile(d, (1024, 512, 256))
    nb = n // tm
    grid_spec = pltpu.PrefetchScalarGridSpec(
        num_scalar_prefetch=2,
        grid=(d // tn, nb),
        in_specs=[pl.BlockSpec((tm, de), lambda j, i, be, nv: (i, 0)),
                  pl.BlockSpec((None, de, tn), lambda j, i, be, nv: (be[i], 0, j)),
                  pl.BlockSpec((None, 1, tn), lambda j, i, be, nv: (be[i], 0, j)),
                  pl.BlockSpec((tm, 1), lambda j, i, be, nv: (i, 0))],
        out_specs=pl.BlockSpec((tm, tn), lambda j, i, be, nv: (i, j)),
        scratch_shapes=[pltpu.VMEM((de, tn), BF16)],
    )
    return pl.pallas_call(
        _moe_down_kernel,
        grid_spec=grid_spec,
        out_shape=jax.ShapeDtypeStruct((n, d), F32),
        compiler_params=_cparams(("arbitrary", "arbitrary")),
        name="moe_down",
    )(block_e, nvalid, act, w_down, b_down.reshape(e, 1, d), slot_w.reshape(n, 1))


def _combine_kernel(dest_ref, ys_ref, lat_ref, g2_ref, fg_ref, o_ref, buf, sem):
    tm = lat_ref.shape[1]
    nrow = TOP_K * tm

    def start(s, carry):
        _row_copy(ys_ref, buf, sem, dest_ref[0, 0, s], s).start()
        return carry
    lax.fori_loop(0, nrow, start, 0)

    def wait(s, carry):
        _row_copy(ys_ref, buf, sem, 0, s).wait()
        return carry
    lax.fori_loop(0, nrow, wait, 0)

    acc = buf[0:tm, :]
    for j in range(1, TOP_K):
        acc = acc + buf[j * tm:(j + 1) * tm, :]
    lat = lat_ref[0] + g2_ref[0] * acc
    ms = jnp.mean(lat * lat, axis=-1, keepdims=True)
    o_ref[0] = lat * lax.rsqrt(ms + EPS) * fg_ref[...]


def _combine(ys, dest, lat, g2, final_g):
    b, l, d = lat.shape
    tm = _pick_tile(l, (64,))
    nt = l // tm
    dest_blk = dest.reshape(b, nt, tm, TOP_K).transpose(0, 1, 3, 2).reshape(b * nt, 1, TOP_K * tm)
    return pl.pallas_call(
        _combine_kernel,
        grid=(b, nt),
        in_specs=[pl.BlockSpec((1, 1, TOP_K * tm), lambda bi, i: (bi * nt + i, 0, 0),
                               memory_space=pltpu.SMEM),
                  pl.BlockSpec(memory_space=pl.ANY),
                  pl.BlockSpec((1, tm, d), lambda bi, i: (bi, i, 0)),
                  pl.BlockSpec((1, 1, d), lambda bi, i: (bi, 0, 0)),
                  pl.BlockSpec((1, d), lambda bi, i: (0, 0))],
        out_specs=pl.BlockSpec((1, tm, d), lambda bi, i: (bi, i, 0)),
        out_shape=jax.ShapeDtypeStruct((b, l, d), F32),
        scratch_shapes=[pltpu.VMEM((TOP_K * tm, d), F32), pltpu.SemaphoreType.DMA(())],
        compiler_params=_cparams(("arbitrary", "arbitrary")),
        name="moe_combine_final_norm",
    )(dest_blk, ys, lat, g2, final_g)


def _route_slots(top_i, top_w, n_experts, tm):
    t = top_i.shape[0]
    n_assign = t * TOP_K
    flat_e = top_i.reshape(-1)
    onehot = (flat_e[:, None] == jnp.arange(n_experts, dtype=jnp.int32)[None, :]).astype(jnp.int32)
    csum = jnp.cumsum(onehot, axis=0)
    counts = csum[-1]
    rank = jnp.sum((csum - 1) * onehot, axis=1)
    padded = (counts + tm - 1) // tm * tm
    pad_end = jnp.cumsum(padded)
    start_pad = pad_end - padded
    dest = start_pad[flat_e] + rank
    n_blocks = n_assign // tm + n_experts
    n_slots = n_blocks * tm
    flat_t = jnp.arange(n_assign, dtype=jnp.int32) // TOP_K
    slot_tok = jnp.zeros((n_slots,), jnp.int32).at[dest].set(flat_t)
    slot_w = jnp.zeros((n_slots,), F32).at[dest].set(top_w.reshape(-1))
    block_start = jnp.arange(n_blocks, dtype=jnp.int32) * tm
    block_e = jnp.minimum(jnp.searchsorted(pad_end, block_start, side='right'),
                          n_experts - 1).astype(jnp.int32)
    nvalid = (pad_end[-1] // tm).astype(jnp.int32).reshape(1)
    return slot_tok, slot_w, dest.astype(jnp.int32), block_e, nvalid


def kernel(x, c, ctx, c_ctx, ada_w, ada_b, norm_mix_g, norm_ffn_g, w_in, ssd_conv_w, ssd_conv_b, ssd_dt_bias, ssd_a_log, ssd_d, ssd_norm_g, rw_mu, rw_w0, rw_w2, rw_a0, rw_a2, rw_k_k, rw_k_a, rw_r_k, rw_ln_g, rw_ln_b, rw_g2, w_br_ssd, w_br_rw, w_out, router_w, router_b, w_gu, b_gu, w_down, b_down, final_g):
    assert ada_w.shape[0] == 1, "single trunk layer"
    b, seq, d = x.shape
    cl = ctx.shape[1]
    t = cl + seq
    h = d // HEAD_DIM
    g = SSD_GROUPS
    r = h // g
    gn = g * SSD_STATE
    xbc = d + 2 * gn
    gate_rank = rw_g2.shape[1]
    n_experts = router_w.shape[2]
    assert d % 256 == 0 and r % 2 == 0 and seq % GRID_W == 0
    tb = _pick_tile(cl, (256, 128))
    assert cl % tb == 0 and seq % tb == 0 and tb % SSD_CHUNK == 0
    nctx = cl // tb

    cond = jnp.concatenate([jax.nn.silu(c), jax.nn.silu(c_ctx)[None, :]], axis=0)
    cond = jnp.pad(cond, ((0, 16 - (b + 1) % 16 if (b + 1) % 16 else 0), (0, 0))).astype(BF16)
    mod = _matmul(cond, ada_w[0], ada_b, F32, cond.shape[0], 512)
    sh1, sc1, g1, sh2, sc2, g2 = [mod[:, i * d:(i + 1) * d] for i in range(6)]
    lat3 = lambda m: m[:b, None, :]
    ctx3 = lambda m: m[b:b + 1, None, :]

    h_c = _norm_mod(ctx, norm_mix_g, ctx3(sh1), ctx3(sc1), BF16)
    h_l = _norm_mod(x, norm_mix_g, lat3(sh1), lat3(sc1), BF16)
    h_all = jnp.concatenate([h_c, h_l], axis=1).reshape(b * t, d)

    w = w_in[0]
    o_dt = d + xbc
    o_rw = o_dt + 2 * h
    o_low = o_rw + 3 * d
    n_low = 4 * LOW_RANK + gate_rank
    o_gate = o_low + n_low
    w_big = jnp.concatenate([w[:, :o_dt], w[:, o_rw:o_low], w[:, o_gate:]], axis=1).astype(BF16)
    low_pad = (-n_low) % 512
    w_small = jnp.concatenate(
        [w[:, o_low:o_gate], jnp.zeros((d, low_pad), F32),
         w[:, o_dt:o_rw], jnp.zeros((d, LOW_RANK - 2 * h), F32)], axis=1).astype(BF16)
    n_big = w_big.shape[1]
    n_small = w_small.shape[1]
    tm_in = _pick_tile(b * t, (1024, 512, 256, 128))
    p_big = _matmul(h_all, w_big, jnp.zeros((1, n_big), F32), BF16, tm_in, 512).reshape(b, t, n_big)
    p_small = _matmul(h_all, w_small, jnp.zeros((1, n_small), F32), F32, tm_in,
                      n_small).reshape(b, t, n_small)
    col_rkv = d + xbc
    col_gate = col_rkv + 3 * d

    xbc_act = _ssd_conv(p_big, ssd_conv_w[0], ssd_conv_b, d, xbc, cl)
    n_lowp = n_low + low_pad
    dt_pad = jnp.zeros((LOW_RANK - 2 * h,), F32)
    dtt = p_small[:, :, n_lowp:].reshape(b, t // SSD_CHUNK, SSD_CHUNK, LOW_RANK).transpose(0, 1, 3, 2)
    pc = jnp.stack([jnp.concatenate([ssd_dt_bias[0].reshape(-1), dt_pad]),
                    jnp.concatenate([-jnp.exp(ssd_a_log[0]).reshape(-1), dt_pad]),
                    jnp.concatenate([ssd_d[0].reshape(-1), dt_pad])], axis=0)
    pr = jnp.broadcast_to(pc[:2, :, None], (2, LOW_RANK, SSD_CHUNK))
    y_ssd = _ssd_scan(xbc_act, p_small, n_lowp, dtt, pc, pr, d, tb, nctx)
    ssd_o = _ssd_output(y_ssd, p_big, ssd_norm_g, d, cl, seq)

    mu = rw_mu[0]
    rkv = _token_shift(p_big, mu[None, :3 * d], col_rkv, 3 * d, cl, BF16)
    mu_low = jnp.pad(mu[3 * d:], (0, low_pad))[None, :]
    low = _token_shift(p_small, mu_low, 0, n_lowp, cl, F32)
    y_rw, bv = _rwkv_scan(rkv, low, rw_w0[0][:, None, :], rw_a0[0][:, None, :], rw_w2[0], rw_a2[0],
                          rw_k_k, rw_k_a, rw_r_k.reshape(1, d), d, tb, nctx)
    gr_pad = n_lowp - 4 * LOW_RANK
    g2p = jnp.pad(rw_g2[0], ((0, gr_pad - gate_rank), (0, 0)))
    rw_o = _rwkv_output(y_rw, bv, low, g2p, rw_ln_g, rw_ln_b, d, cl, seq)

    m = _merge(ssd_o, rw_o, w_br_ssd[0], w_br_rw[0], p_big, col_gate, cl)
    lat = _outproj(m, w_out[0], x, lat3(g1))

    h2 = _norm_mod(lat, norm_ffn_g, lat3(sh2), lat3(sc2), F32).reshape(b * seq, d)
    rw_pad = jnp.pad(router_w[0], ((0, 0), (0, 128 - n_experts))).astype(BF16)
    rb_pad = jnp.pad(router_b, ((0, 0), (0, 128 - n_experts)), constant_values=-1e30)
    idx, wts = _router(h2, rw_pad, rb_pad)
    slot_tok, slot_w, dest, block_e, nvalid = _route_slots(idx[:, :TOP_K], wts[:, :TOP_K],
                                                           n_experts, MOE_TM)
    xs = _gather_rows(h2, slot_tok, MOE_TM)
    act = _moe_up(xs, w_gu[0], b_gu[0], block_e, nvalid, MOE_TM)
    ys = _moe_down(act, w_down[0], b_down[0], slot_w, block_e, nvalid, MOE_TM)
    return _combine(ys, dest.reshape(b, seq, TOP_K), lat, lat3(g2), final_g[None, :])
```
